```python
import math
import jax
import jax.numpy as jnp
from jax import lax
import numpy as np

D_MODEL = 1024
BATCH = 4
SEQ = 8192
DEPTH = 4

N_EVEN = (DEPTH + 1) // 2
N_ODD = DEPTH // 2
RMS_EPS = 1e-6
L2_EPS = 1e-6
D_FF = ((8 * D_MODEL + 3 * 256 - 1) // (3 * 256)) * 256

A_WIDTH = 256
A_CONV = 3
GDN_HEADS = 6
GDN_HEAD_DIM = 128
GDN_WIDTH = GDN_HEADS * GDN_HEAD_DIM
GDN_CONV = 4
GDN_CHUNK = 64
EV_SPLITS = (A_WIDTH, A_WIDTH, A_WIDTH, 3 * GDN_WIDTH, GDN_WIDTH, GDN_HEADS, GDN_HEADS)
EV_IN_COLS = sum(EV_SPLITS)
EV_MIX_WIDTH = A_WIDTH + GDN_WIDTH

RWKV_HEADS = 8
RWKV_HEAD_DIM = 64
RWKV_WIDTH = RWKV_HEADS * RWKV_HEAD_DIM
RWKV_W_LORA = 64
RWKV_A_LORA = 64
RWKV_V_LORA = 32
RWKV_G_LORA = 128
RWKV_LN_EPS = 64e-5
RWKV_SPLITS = (RWKV_WIDTH, RWKV_WIDTH, RWKV_WIDTH, RWKV_W_LORA, RWKV_A_LORA, RWKV_G_LORA)
RWKV_SHIFT_COLS = sum(RWKV_SPLITS)
MLA_HEADS = 8
MLA_NOPE = 64
MLA_ROPE = 32
MLA_V = 64
MLA_QK_DIM = MLA_NOPE + MLA_ROPE
MLA_Q_LORA = 512
MLA_KV_LORA = 256
MLA_WIDTH = MLA_HEADS * MLA_V
MLA_SPLITS = (MLA_Q_LORA, MLA_KV_LORA, MLA_ROPE)
MLA_IN_COLS = sum(MLA_SPLITS)
OD_IN_COLS = RWKV_SHIFT_COLS + MLA_IN_COLS
OD_MIX_WIDTH = RWKV_WIDTH + MLA_WIDTH
ROPE_THETA = 10000.0
Q_BLOCK = 128

kernel_name = 'hybrid_conv_gdn_rwkv7_mla_trunk'


def split_cols(z, sizes):
    return jnp.split(z, [int(s) for s in np.cumsum(sizes)[:-1]], axis=-1)


def rms_norm(x, gain, eps=RMS_EPS):
    xf = x.astype(jnp.float32)
    y = xf * lax.rsqrt(jnp.mean(xf * xf, axis=-1, keepdims=True) + eps)
    return (y * gain.astype(jnp.float32)).astype(x.dtype)


def l2_normalize(x):
    xf = x.astype(jnp.float32)
    return xf * lax.rsqrt(jnp.sum(xf * xf, axis=-1, keepdims=True) + L2_EPS)


def token_shift(z):
    return jnp.pad(z, ((0, 0), (1, 0), (0, 0)))[:, :-1]


def shift_lerp(z, mu):
    return z + mu * (token_shift(z) - z)


def causal_depthwise_conv(x, w):
    K = w.shape[0]
    T = x.shape[1]
    xp = jnp.pad(x, ((0, 0), (K - 1, 0), (0, 0)))
    y = xp[:, 0:T] * w[0]
    for j in range(1, K):
        y = y + xp[:, j:j + T] * w[j]
    return y


def swiglu(xn, w_gu, w_down):
    gate, up = jnp.split(xn @ w_gu, 2, axis=-1)
    return (jax.nn.silu(gate) * up) @ w_down


def rope_tables(positions):
    inv_freq = ROPE_THETA ** (-jnp.arange(0, MLA_ROPE, 2, dtype=jnp.float32) / MLA_ROPE)
    ang = positions.astype(jnp.float32)[..., None] * inv_freq
    return jnp.cos(ang), jnp.sin(ang)


def apply_rope(x, cos, sin):
    x = x.astype(jnp.float32)
    x1, x2 = jnp.split(x, 2, axis=-1)
    c = cos[:, :, None, :]
    s = sin[:, :, None, :]
    return jnp.concatenate([x1 * c - x2 * s, x2 * c + x1 * s], axis=-1)


def unit_lower_inverse(L):
    C = L.shape[-1]
    cols = jnp.arange(C)

    def body(i, A):
        row = A[..., i, :]
        upd = row + jnp.einsum('...j,...jk->...k', row, A)
        return A.at[..., i, :].set(jnp.where(cols < i, upd, row))

    A = lax.fori_loop(1, C, body, -L)
    return A + jnp.eye(C, dtype=L.dtype)


def to_chunks(t):
    b, T, h = t.shape[:3]
    t = t.reshape(b, T // GDN_CHUNK, GDN_CHUNK, h, *t.shape[3:])
    return jnp.moveaxis(t, 3, 1)


def gated_delta_rule_chunked(q, k, v, g, beta):
    B_, T, H, dk = q.shape
    dv = v.shape[-1]
    q = to_chunks(q * dk ** -0.5)
    k = to_chunks(k)
    v = to_chunks(v)
    beta = to_chunks(beta)
    g = jnp.cumsum(to_chunks(g), axis=-1)
    idx = jnp.arange(GDN_CHUNK)
    causal = idx[:, None] >= idx[None, :]
    strict = idx[:, None] > idx[None, :]
    diff = g[..., :, None] - g[..., None, :]
    decay = jnp.where(causal, jnp.exp(jnp.where(causal, diff, 0.0)), 0.0)
    kb = k * beta[..., None]
    L = jnp.where(strict, jnp.einsum('bhncd,bhnsd->bhncs', kb, k) * decay, 0.0)
    Tm = unit_lower_inverse(L)
    u = jnp.einsum('bhncs,bhnse->bhnce', Tm, v * beta[..., None])
    w = jnp.einsum('bhncs,bhnsd->bhncd', Tm, kb * jnp.exp(g)[..., None])
    qk = jnp.einsum('bhncd,bhnsd->bhncs', q, k) * decay
    q_dec = q * jnp.exp(g)[..., None]
    k_dec = k * jnp.exp(g[..., -1:] - g)[..., None]
    g_tot = jnp.exp(g[..., -1])
    xs = tuple(jnp.moveaxis(t, 2, 0) for t in (u, w, q_dec, k_dec, qk, g_tot))

    def step(S, inp):
        u_n, w_n, qd_n, kd_n, qk_n, gt_n = inp
        v_new = u_n - jnp.einsum('bhcd,bhde->bhce', w_n, S)
        o_n = jnp.einsum('bhcd,bhde->bhce', qd_n, S) + jnp.einsum('bhcs,bhse->bhce', qk_n, v_new)
        S = S * gt_n[..., None, None] + jnp.einsum('bhcd,bhce->bhde', kd_n, v_new)
        return S, o_n

    S0 = jnp.zeros((B_, H, dk, dv), jnp.float32)
    _, o = lax.scan(step, S0, xs)
    o = jnp.moveaxis(o, 0, 2)
    return jnp.moveaxis(o, 1, 3).reshape(B_, T, H, dv)


def rwkv7_scan(r, decay, k, v, kk, a):
    B_, T, H, N = r.shape

    def step(S, inp):
        r_t, w_t, k_t, v_t, kk_t, a_t = inp
        sa = jnp.einsum('bhvk,bhk->bhv', S, -kk_t)
        S = S * w_t[:, :, None, :] + sa[..., :, None] * (kk_t * a_t)[..., None, :] + v_t[..., :, None] * k_t[..., None, :]
        return S, jnp.einsum('bhvk,bhk->bhv', S, r_t)

    xs = tuple(jnp.moveaxis(t.astype(jnp.float32), 1, 0) for t in (r, decay, k, v, kk, a))
    S0 = jnp.zeros((B_, H, N, N), jnp.float32)
    _, y = lax.scan(step, S0, xs)
    return jnp.moveaxis(y, 0, 1)


def blocked_causal_attention(q, k, v):
    B_, T, H, d = q.shape
    dv = v.shape[-1]
    nb = T // Q_BLOCK
    scale = d ** -0.5
    qb = q.reshape(B_, nb, Q_BLOCK, H, d).transpose(1, 0, 3, 2, 4)
    kf = k.transpose(0, 2, 1, 3)
    vf = v.transpose(0, 2, 1, 3)
    key_pos = jnp.arange(T)

    def one_block(args):
        q_blk, blk = args
        s = jnp.einsum('bhqd,bhkd->bhqk', q_blk, kf).astype(jnp.float32) * scale
        q_pos = blk * Q_BLOCK + jnp.arange(Q_BLOCK)
        s = jnp.where(key_pos[None, :] <= q_pos[:, None], s, -jnp.inf)
        p = jax.nn.softmax(s, axis=-1)
        return jnp.einsum('bhqk,bhkd->bhqd', p, vf)

    o = lax.map(one_block, (qb, jnp.arange(nb)))
    return o.transpose(1, 0, 3, 2, 4).reshape(B_, T, H, dv)


def even_mixer(xn, w_in, conv_a, conv_qkv, a_log, dt_bias, out_norm, w_out):
    B_, T, _ = xn.shape
    a_b, a_c, a_h, qkv, gate_z, b_raw, a_raw = split_cols(xn @ w_in, EV_SPLITS)
    y_a = a_b * causal_depthwise_conv(a_c * a_h, conv_a)
    qkv = jax.nn.silu(causal_depthwise_conv(qkv, conv_qkv))
    q, k, v = [t.reshape(B_, T, GDN_HEADS, GDN_HEAD_DIM) for t in jnp.split(qkv, 3, axis=-1)]
    beta = jax.nn.sigmoid(b_raw.astype(jnp.float32))
    g = -jnp.exp(a_log.astype(jnp.float32)) * jax.nn.softplus(a_raw.astype(jnp.float32) + dt_bias.astype(jnp.float32))
    o = gated_delta_rule_chunked(l2_normalize(q), l2_normalize(k), v.astype(jnp.float32), g, beta)
    o = rms_norm(o, out_norm) * jax.nn.silu(gate_z.reshape(B_, T, GDN_HEADS, GDN_HEAD_DIM).astype(jnp.float32))
    y = jnp.concatenate([y_a, o.reshape(B_, T, GDN_WIDTH).astype(xn.dtype)], axis=-1)
    return y @ w_out


def odd_mixer(xn, cos, sin, v_first, w_in, shift_mu, w0, w2, a0, a2, g2, k_k, k_a, r_k,
              lnx_w, lnx_b, qa_norm, kva_norm, w_uq, w_ukv, q_ln, k_ln, w_out, vres):
    B_, T, _ = xn.shape
    if vres is not None:
        w_in = jnp.concatenate([w_in, vres[0]], axis=1)
    z = xn @ w_in
    z_rwkv = shift_lerp(z[..., :RWKV_SHIFT_COLS], shift_mu).astype(jnp.float32)
    r, k, v, wd, ad, gd = split_cols(z_rwkv, RWKV_SPLITS)
    w_log = -jax.nn.softplus(-(w0 + jnp.tanh(wd) @ w2)) - 0.5
    decay = jnp.exp(-jnp.exp(w_log))
    a = jax.nn.sigmoid(a0 + ad @ a2)
    g = jax.nn.sigmoid(gd) @ g2
    if vres is None:
        v_first = v
    else:
        vd = shift_lerp(z[..., OD_IN_COLS:], vres[1]).astype(jnp.float32)
        v = v + (v_first - v) * jax.nn.sigmoid(vres[2] + vd @ vres[3])
    hshape = (B_, T, RWKV_HEADS, RWKV_HEAD_DIM)
    kk = l2_normalize((k * k_k).reshape(hshape))
    k = k * (1.0 + (a - 1.0) * k_a)
    r_h, k_h, v_h = r.reshape(hshape), k.reshape(hshape), v.reshape(hshape)
    y = rwkv7_scan(r_h, decay.reshape(hshape), k_h, v_h, kk, a.reshape(hshape))
    mu = jnp.mean(y, axis=-1, keepdims=True)
    var = jnp.mean(jnp.square(y - mu), axis=-1, keepdims=True)
    y = ((y - mu) * lax.rsqrt(var + RWKV_LN_EPS)).reshape(B_, T, RWKV_WIDTH) * lnx_w + lnx_b
    y = y + (jnp.sum(r_h * k_h * r_k, axis=-1, keepdims=True) * v_h).reshape(B_, T, RWKV_WIDTH)
    y_rwkv = y * g
    cq, ckv, k_rope = split_cols(z[..., RWKV_SHIFT_COLS:OD_IN_COLS], MLA_SPLITS)
    q = (rms_norm(cq, qa_norm) @ w_uq).reshape(B_, T, MLA_HEADS, MLA_QK_DIM)
    kv = (rms_norm(ckv, kva_norm) @ w_ukv).reshape(B_, T, MLA_HEADS, MLA_NOPE + MLA_V)
    k_nope, v_mla = jnp.split(kv, [MLA_NOPE], axis=-1)
    k_rope_h = jnp.broadcast_to(k_rope[:, :, None, :], (B_, T, MLA_HEADS, MLA_ROPE))
    k_mla = rms_norm(jnp.concatenate([k_nope, k_rope_h], axis=-1), k_ln)
    q = rms_norm(q, q_ln)
    q = jnp.concatenate([q[..., :MLA_NOPE].astype(jnp.float32), apply_rope(q[..., MLA_NOPE:], cos, sin)], axis=-1)
    k_mla = jnp.concatenate([k_mla[..., :MLA_NOPE].astype(jnp.float32), apply_rope(k_mla[..., MLA_NOPE:], cos, sin)], axis=-1)
    o = blocked_causal_attention(q, k_mla, v_mla.astype(jnp.float32))
    y_mix = jnp.concatenate([y_rwkv.astype(xn.dtype), o.reshape(B_, T, MLA_WIDTH).astype(xn.dtype)], axis=-1)
    return y_mix @ w_out, v_first


def setup_inputs(seed: int = 0) -> dict:
    key = jax.random.key(seed)
    keys = jax.random.split(key, 48)
    counter = [0]

    def nk():
        counter[0] += 1
        return keys[counter[0] - 1]

    def nrm(shape, scale):
        return jax.random.normal(nk(), shape, jnp.float32) * scale

    def gain(shape):
        return 1.0 + nrm(shape, 0.02)

    def unif(shape, lo, hi):
        return jax.random.uniform(nk(), shape, jnp.float32, lo, hi)

    NE, NO, NV = N_EVEN, N_ODD, N_ODD - 1
    x = jax.random.normal(nk(), (BATCH, SEQ, D_MODEL), jnp.float32)
    positions = jax.random.randint(nk(), (BATCH, 1), 0, 1024, dtype=jnp.int32) + jnp.arange(SEQ, dtype=jnp.int32)[None, :]
    dt = jnp.exp(unif((NE, GDN_HEADS), math.log(1e-3), math.log(1e-1)))
    return {
        'x': x,
        'positions': positions,
        'norm_mix': gain((DEPTH, D_MODEL)),
        'norm_ffn': gain((DEPTH, D_MODEL)),
        'ffn_w_gu': nrm((DEPTH, D_MODEL, 2 * D_FF), D_MODEL ** -0.5),
        'ffn_w_down': nrm((DEPTH, D_FF, D_MODEL), D_FF ** -0.5),
        'ev_w_in': nrm((NE, D_MODEL, EV_IN_COLS), D_MODEL ** -0.5),
        'ev_conv_a': nrm((NE, A_CONV, A_WIDTH), A_CONV ** -0.5),
        'ev_conv_qkv': nrm((NE, GDN_CONV, 3 * GDN_WIDTH), GDN_CONV ** -0.5),
        'ev_a_log': jnp.log(unif((NE, GDN_HEADS), 1.0, 16.0)),
        'ev_dt_bias': dt + jnp.log(-jnp.expm1(-dt)),
        'ev_out_norm': gain((NE, GDN_HEAD_DIM)),
        'ev_w_out': nrm((NE, EV_MIX_WIDTH, D_MODEL), EV_MIX_WIDTH ** -0.5),
        'od_w_in': nrm((NO, D_MODEL, OD_IN_COLS), D_MODEL ** -0.5),
        'od_shift_mu': unif((NO, RWKV_SHIFT_COLS), 0.0, 1.0),
        'od_w0': unif((NO, RWKV_WIDTH), -6.0, -1.0),
        'od_w2': nrm((NO, RWKV_W_LORA, RWKV_WIDTH), 0.1),
        'od_a0': nrm((NO, RWKV_WIDTH), 0.1),
        'od_a2': nrm((NO, RWKV_A_LORA, RWKV_WIDTH), 0.5 * RWKV_A_LORA ** -0.5),
        'od_g2': nrm((NO, RWKV_G_LORA, RWKV_WIDTH), RWKV_G_LORA ** -0.5),
        'od_k_k': 0.85 + nrm((NO, RWKV_WIDTH), 0.05),
        'od_k_a': 1.0 + nrm((NO, RWKV_WIDTH), 0.05),
        'od_r_k': nrm((NO, RWKV_HEADS, RWKV_HEAD_DIM), 0.1),
        'od_lnx_w': gain((NO, RWKV_WIDTH)),
        'od_lnx_b': nrm((NO, RWKV_WIDTH), 0.01),
        'od_vres_w1': nrm((NV, D_MODEL, RWKV_V_LORA), D_MODEL ** -0.5),
        'od_vres_mu': unif((NV, RWKV_V_LORA), 0.0, 1.0),
        'od_vres_v0': nrm((NV, RWKV_WIDTH), 0.1),
        'od_vres_v2': nrm((NV, RWKV_V_LORA, RWKV_WIDTH), 0.5 * RWKV_V_LORA ** -0.5),
        'od_qa_norm': gain((NO, MLA_Q_LORA)),
        'od_kva_norm': gain((NO, MLA_KV_LORA)),
        'od_w_uq': nrm((NO, MLA_Q_LORA, MLA_HEADS * MLA_QK_DIM), MLA_Q_LORA ** -0.5),
        'od_w_ukv': nrm((NO, MLA_KV_LORA, MLA_HEADS * (MLA_NOPE + MLA_V)), MLA_KV_LORA ** -0.5),
        'od_q_ln': gain((NO, MLA_QK_DIM)),
        'od_k_ln': gain((NO, MLA_QK_DIM)),
        'od_w_out': nrm((NO, OD_MIX_WIDTH, D_MODEL), OD_MIX_WIDTH ** -0.5),
    }


def reference(x, positions, norm_mix, norm_ffn, ffn_w_gu, ffn_w_down,
              ev_w_in, ev_conv_a, ev_conv_qkv, ev_a_log, ev_dt_bias, ev_out_norm, ev_w_out,
              od_w_in, od_shift_mu, od_w0, od_w2, od_a0, od_a2, od_g2, od_k_k, od_k_a, od_r_k,
              od_lnx_w, od_lnx_b, od_vres_w1, od_vres_mu, od_vres_v0, od_vres_v2,
              od_qa_norm, od_kva_norm, od_w_uq, od_w_ukv, od_q_ln, od_k_ln, od_w_out):
    cos, sin = rope_tables(positions)
    v_first = None
    for layer in range(DEPTH):
        xn = rms_norm(x, norm_mix[layer])
        if layer % 2 == 0:
            e = layer // 2
            h = even_mixer(xn, ev_w_in[e], ev_conv_a[e], ev_conv_qkv[e], ev_a_log[e],
                           ev_dt_bias[e], ev_out_norm[e], ev_w_out[e])
        else:
            o = layer // 2
            vres = None if o == 0 else (od_vres_w1[o - 1], od_vres_mu[o - 1], od_vres_v0[o - 1], od_vres_v2[o - 1])
            h, v_first = odd_mixer(xn, cos, sin, v_first, od_w_in[o], od_shift_mu[o], od_w0[o], od_w2[o],
                                   od_a0[o], od_a2[o], od_g2[o], od_k_k[o], od_k_a[o], od_r_k[o],
                                   od_lnx_w[o], od_lnx_b[o], od_qa_norm[o], od_kva_norm[o], od_w_uq[o],
                                   od_w_ukv[o], od_q_ln[o], od_k_ln[o], od_w_out[o], vres)
        x = x + h
        x = x + swiglu(rms_norm(x, norm_ffn[layer]), ffn_w_gu[layer], ffn_w_down[layer])
    return x
```

```python
import functools
import math

import jax
import jax.numpy as jnp
from jax import lax
from jax.experimental import pallas as pl
from jax.experimental.pallas import tpu as pltpu

F32 = jnp.float32
BF16 = jnp.bfloat16
HIGHEST = lax.Precision.HIGHEST

LANES = 128
SUBLANES = 8
VMEM_LIMIT = 56 * 1024 * 1024

RMS_EPS = 1e-6
L2_EPS = 1e-6
RWKV_LN_EPS = 64e-5
ROPE_THETA = 10000.0

A_WIDTH = 256
GDN_HEADS = 6
GDN_DIM = 128
GDN_WIDTH = GDN_HEADS * GDN_DIM
GDN_CHUNK = 64
GDN_BLOCK = 16

RWKV_HEADS = 8
RWKV_DIM = 64
RWKV_WIDTH = RWKV_HEADS * RWKV_DIM
RWKV_PAIRS = RWKV_WIDTH // LANES
RWKV_W_LORA = 64
RWKV_A_LORA = 64
RWKV_V_LORA = 32
RWKV_G_LORA = 128
RWKV_SHIFT_COLS = 3 * RWKV_WIDTH + RWKV_W_LORA + RWKV_A_LORA + RWKV_G_LORA

MLA_HEADS = 8
MLA_NOPE = 64
MLA_ROPE = 32
MLA_V = 64
MLA_QK = MLA_NOPE + MLA_ROPE
MLA_Q_LORA = 512
MLA_KV_LORA = 256
MLA_PAD = 128

HALO = SUBLANES


def _dot(a, b, precision=None):
    return jnp.dot(a, b, preferred_element_type=F32, precision=precision)


def _dot_nt(a, b, precision=None):
    return lax.dot_general(a, b, (((1,), (1,)), ((), ())), preferred_element_type=F32, precision=precision)


def _dot_tn(a, b, precision=None):
    return lax.dot_general(a, b, (((0,), (0,)), ((), ())), preferred_element_type=F32, precision=precision)


def _rms(x, gain):
    return x * lax.rsqrt(jnp.mean(x * x, axis=-1, keepdims=True) + RMS_EPS) * gain


def _sigmoid(x):
    return 1.0 / (1.0 + jnp.exp(-x))


def _silu(x):
    return x * _sigmoid(x)


def _softplus(x):
    return jnp.maximum(x, 0.0) + jnp.log(1.0 + jnp.exp(-jnp.abs(x)))


def _seg64_sum(x):
    lo = lax.broadcasted_iota(jnp.int32, x.shape, 1) < RWKV_DIM
    s_lo = jnp.sum(jnp.where(lo, x, 0.0), axis=-1, keepdims=True)
    s_hi = jnp.sum(jnp.where(lo, 0.0, x), axis=-1, keepdims=True)
    return jnp.where(lo, s_lo, s_hi)


def _shifted_rows(z, scr_ref, carry_ref, first, shifts):
    tm = z.shape[0]
    prev = carry_ref[...]
    scr_ref[0:HALO, :] = jnp.where(first, jnp.zeros_like(prev), prev)
    scr_ref[HALO:HALO + tm, :] = z
    carry_ref[...] = z[tm - HALO:, :]
    return [scr_ref[pl.ds(HALO - s, tm), :] for s in shifts]


def _rope_kernel(pos_ref, invf_ref, c_ref, sa_ref, sb_ref):
    ang = pos_ref[...] * invf_ref[...]
    lane = lax.broadcasted_iota(jnp.int32, ang.shape, 1)
    cos = jnp.cos(ang)
    sin = jnp.sin(ang)
    half = MLA_ROPE // 2
    c_ref[...] = jnp.where(lane < MLA_NOPE, 1.0, jnp.where(lane < MLA_QK, cos, 0.0))
    sa_ref[...] = jnp.where((lane >= MLA_NOPE) & (lane < MLA_NOPE + half), -sin, 0.0)
    sb_ref[...] = jnp.where((lane >= MLA_NOPE + half) & (lane < MLA_QK), sin, 0.0)


def _rope_tables(positions, tm):
    n = positions.size
    pos = jnp.broadcast_to(positions.reshape(n, 1).astype(F32), (n, LANES))
    inv_freq = ROPE_THETA ** (-jnp.arange(0, MLA_ROPE, 2, dtype=F32) / MLA_ROPE)
    half = MLA_ROPE // 2
    invf = jnp.zeros((1, LANES), F32)
    invf = invf.at[0, MLA_NOPE:MLA_NOPE + half].set(inv_freq).at[0, MLA_NOPE + half:MLA_QK].set(inv_freq)
    spec = pl.BlockSpec((tm, LANES), lambda i: (i, 0))
    out = jax.ShapeDtypeStruct((n, LANES), F32)
    return pl.pallas_call(
        _rope_kernel,
        grid=(n // tm,),
        in_specs=[spec, pl.BlockSpec((1, LANES), lambda i: (0, 0))],
        out_specs=[spec, spec, spec],
        out_shape=[out, out, out],
        name="rope_tables",
    )(pos, invf)


def _ev_in_kernel(x_ref, gain_ref, w_ref, ws_ref, ca_ref, cq_ref, alog_ref, dtb_ref,
                  ya_ref, q_ref, k_ref, v_ref, gate_ref, bg_ref,
                  scr_ref, carry_p_ref, carry_qkv_ref):
    first = pl.program_id(1) == 0
    xn = _rms(x_ref[...], gain_ref[...]).astype(BF16)

    za = _dot(xn, w_ref[:, 0:3 * A_WIDTH])
    a_b = za[:, 0:A_WIDTH]
    prod = za[:, A_WIDTH:2 * A_WIDTH] * za[:, 2 * A_WIDTH:3 * A_WIDTH]
    p1, p2 = _shifted_rows(prod, scr_ref.at[:, 0:A_WIDTH], carry_p_ref, first, (1, 2))
    ca = ca_ref[...]
    ya = a_b * (ca[0:1, :] * p2 + ca[1:2, :] * p1 + ca[2:3, :] * prod)
    ya_ref[...] = ya.astype(ya_ref.dtype)

    base = 3 * A_WIDTH
    for g, out_ref in enumerate((q_ref, k_ref, v_ref)):
        cols = slice(base + g * GDN_WIDTH, base + (g + 1) * GDN_WIDTH)
        z = _dot(xn, w_ref[:, cols])
        z1, z2, z3 = _shifted_rows(z, scr_ref, carry_qkv_ref.at[g], first, (1, 2, 3))
        cw = cq_ref[:, g * GDN_WIDTH:(g + 1) * GDN_WIDTH]
        y = _silu(cw[0:1, :] * z3 + cw[1:2, :] * z2 + cw[2:3, :] * z1 + cw[3:4, :] * z)
        for h in range(GDN_HEADS):
            hs = slice(h * GDN_DIM, (h + 1) * GDN_DIM)
            yh = y[:, hs]
            if g < 2:
                yh = yh * lax.rsqrt(jnp.sum(yh * yh, axis=-1, keepdims=True) + L2_EPS)
            if g == 0:
                yh = yh * (GDN_DIM ** -0.5)
            out_ref[:, hs] = yh

    gcols = slice(base + 3 * GDN_WIDTH, base + 4 * GDN_WIDTH)
    gate_ref[...] = _dot(xn, w_ref[:, gcols])

    zs = _dot(xn, ws_ref[...])
    lane = lax.broadcasted_iota(jnp.int32, zs.shape, 1)
    g_log = -jnp.exp(alog_ref[...]) * _softplus(zs + dtb_ref[...])
    bg_ref[...] = jnp.where(lane < GDN_HEADS, _sigmoid(zs), g_log)


def _ev_in(x2, gain, w_main, w_small, conv_a, conv_qkv, alog_l, dtb_l, bsz, tlen, tm):
    n, d = x2.shape
    nt = tlen // tm
    row = lambda b, i: (b * nt + i, 0)
    const = lambda b, i: (0, 0)
    rs = lambda w: pl.BlockSpec((tm, w), row)
    full = lambda a: pl.BlockSpec(a.shape, const)
    outs = [
        jax.ShapeDtypeStruct((n, A_WIDTH), BF16),
        jax.ShapeDtypeStruct((n, GDN_WIDTH), F32),
        jax.ShapeDtypeStruct((n, GDN_WIDTH), F32),
        jax.ShapeDtypeStruct((n, GDN_WIDTH), F32),
        jax.ShapeDtypeStruct((n, GDN_WIDTH), F32),
        jax.ShapeDtypeStruct((n, LANES), F32),
    ]
    return pl.pallas_call(
        _ev_in_kernel,
        grid=(bsz, nt),
        in_specs=[rs(d), full(gain), full(w_main), full(w_small), full(conv_a), full(conv_qkv),
                  full(alog_l), full(dtb_l)],
        out_specs=[rs(A_WIDTH), rs(GDN_WIDTH), rs(GDN_WIDTH), rs(GDN_WIDTH), rs(GDN_WIDTH), rs(LANES)],
        out_shape=outs,
        scratch_shapes=[
            pltpu.VMEM((HALO + tm, GDN_WIDTH), F32),
            pltpu.VMEM((HALO, A_WIDTH), F32),
            pltpu.VMEM((3, HALO, GDN_WIDTH), F32),
        ],
        compiler_params=pltpu.CompilerParams(
            dimension_semantics=("arbitrary", "arbitrary"), vmem_limit_bytes=VMEM_LIMIT),
        name="even_in_proj",
    )(x2, gain, w_main, w_small, conv_a, conv_qkv, alog_l, dtb_l)


def _unit_lower_inverse(l_mat, row, col):
    eye = (row == col).astype(F32)
    in_block = (row // GDN_BLOCK) == (col // GDN_BLOCK)
    l_diag = jnp.where(in_block, l_mat, 0.0)
    l_off = jnp.where(in_block, 0.0, l_mat)
    dot = functools.partial(_dot, precision=HIGHEST)
    x2 = dot(l_diag, l_diag)
    x4 = dot(x2, x2)
    x8 = dot(x4, x4)
    d_inv = eye - l_diag
    d_inv = d_inv + dot(d_inv, x2)
    d_inv = d_inv + dot(d_inv, x4)
    d_inv = d_inv + dot(d_inv, x8)
    n1 = dot(d_inv, l_off)
    n2 = dot(n1, n1)
    outer = eye - n1
    outer = outer + dot(outer, n2)
    return dot(outer, d_inv)


def _gdn_kernel(q_ref, k_ref, v_ref, bg_ref, bgt_ref, gate_ref, onorm_ref, o_ref, s_ref, *, nchunk):
    @pl.when(pl.program_id(1) == 0)
    def _():
        s_ref[...] = jnp.zeros_like(s_ref)

    c = GDN_CHUNK
    row = lax.broadcasted_iota(jnp.int32, (c, c), 0)
    col = lax.broadcasted_iota(jnp.int32, (c, c), 1)
    causal = row >= col
    strict = row > col
    tri = causal.astype(F32)
    tri_t = (row <= col).astype(F32)
    dot = functools.partial(_dot, precision=HIGHEST)
    dot_nt = functools.partial(_dot_nt, precision=HIGHEST)
    dot_tn = functools.partial(_dot_tn, precision=HIGHEST)
    onorm = onorm_ref[...]

    for ci in range(nchunk):
        rows = slice(ci * c, (ci + 1) * c)
        bg = bg_ref[rows, :]
        g_cols = dot(tri, bg)
        g_rows = dot(bgt_ref[0, ci], tri_t)
        for h in range(GDN_HEADS):
            hs = slice(h * GDN_DIM, (h + 1) * GDN_DIM)
            q = q_ref[rows, hs]
            k = k_ref[rows, hs]
            v = v_ref[rows, hs]
            beta = bg[:, h:h + 1]
            gc = g_cols[:, GDN_HEADS + h:GDN_HEADS + h + 1]
            gr = g_rows[GDN_HEADS + h:GDN_HEADS + h + 1, :]
            g_last = gc[c - 1:c, :]
            decay = jnp.where(causal, jnp.exp(jnp.where(causal, gc - gr, 0.0)), 0.0)
            kb = k * beta
            l_mat = jnp.where(strict, dot_nt(kb, k) * decay, 0.0)
            t_inv = _unit_lower_inverse(l_mat, row, col)
            eg = jnp.exp(gc)
            u = dot(t_inv, v * beta)
            w = dot(t_inv, kb * eg)
            qk = dot_nt(q, k) * decay
            state = s_ref[h]
            v_new = u - dot(w, state)
            o = dot(q * eg, state) + dot(qk, v_new)
            k_dec = k * jnp.exp(g_last - gc)
            s_ref[h] = state * jnp.exp(g_last) + dot_tn(k_dec, v_new)
            o = _rms(o, onorm) * _silu(gate_ref[rows, hs])
            o_ref[rows, hs] = o.astype(o_ref.dtype)


def _gdn(q, k, v, bg, bgt, gate, onorm, bsz, tlen, tt):
    n = q.shape[0]
    nt = tlen // tt
    nchunk = tt // GDN_CHUNK
    row = lambda b, i: (b * nt + i, 0)
    rs = lambda w: pl.BlockSpec((tt, w), row)
    return pl.pallas_call(
        functools.partial(_gdn_kernel, nchunk=nchunk),
        grid=(bsz, nt),
        in_specs=[rs(GDN_WIDTH), rs(GDN_WIDTH), rs(GDN_WIDTH), rs(LANES),
                  pl.BlockSpec((1, nchunk, 2 * SUBLANES, GDN_CHUNK), lambda b, i: (b * nt + i, 0, 0, 0)),
                  rs(GDN_WIDTH), pl.BlockSpec(onorm.shape, lambda b, i: (0, 0))],
        out_specs=rs(GDN_WIDTH),
        out_shape=jax.ShapeDtypeStruct((n, GDN_WIDTH), BF16),
        scratch_shapes=[pltpu.VMEM((GDN_HEADS, GDN_DIM, GDN_DIM), F32)],
        compiler_params=pltpu.CompilerParams(
            dimension_semantics=("arbitrary", "arbitrary"), vmem_limit_bytes=VMEM_LIMIT),
        name="gated_delta_rule",
    )(q, k, v, bg, bgt, gate, onorm)


OD_R = 0
OD_K = RWKV_WIDTH
OD_V = 2 * RWKV_WIDTH
OD_WA = 3 * RWKV_WIDTH
OD_G = OD_WA + LANES
OD_X = RWKV_SHIFT_COLS
OD_SMALL = 3 * LANES
OD_CQ = OD_X + LANES
OD_CKV = OD_CQ + MLA_Q_LORA
OD_COLS = OD_CKV + MLA_KV_LORA


def _od_in_kernel(*refs, has_vres):
    (x_ref, gain_ref, w_ref, mu_ref, wa2_ref, w0_ref, a0_ref, g2_ref, kk_ref, ka_ref,
     qan_ref, kvan_ref, wuq_ref, wuk_ref, wuv_ref, qln_ref, kln_ref,
     c_ref, sa_ref, sb_ref) = refs[:20]
    pos = 20
    if has_vres:
        vfirst_ref, v0_ref, v2_ref = refs[pos:pos + 3]
        pos += 3
    (r_out, w_out, k_out, v_out, kkn_out, b_out, g_out, q_out, kmla_out, vmla_out) = refs[pos:pos + 10]
    scr_ref, carry_ref = refs[pos + 10:]

    first = pl.program_id(1) == 0
    xn = _rms(x_ref[...], gain_ref[...]).astype(BF16)

    def lerped(start, width, slot):
        z = _dot(xn, w_ref[:, start:start + width])
        (zp,) = _shifted_rows(z, scr_ref.at[:, 0:width], carry_ref.at[:, slot:slot + width], first, (1,))
        return z, z + mu_ref[:, start:start + width] * (zp - z)

    _, r = lerped(OD_R, RWKV_WIDTH, OD_R)
    r_out[...] = r

    small, small_l = lerped(OD_WA, OD_SMALL, OD_WA)
    wa = small_l[:, 0:LANES]
    gd = small_l[:, LANES:2 * LANES]
    zx = small[:, 2 * LANES:3 * LANES]
    zx_l = small_l[:, 2 * LANES:3 * LANES]
    lane = lax.broadcasted_iota(jnp.int32, wa.shape, 1)
    lora_in = jnp.where(lane < RWKV_W_LORA, jnp.tanh(wa), wa).astype(BF16)
    lora = _dot(lora_in, wa2_ref[...])
    w_log = -_softplus(-(w0_ref[...] + lora[:, 0:RWKV_WIDTH])) - 0.5
    w_out[...] = jnp.exp(-jnp.exp(w_log))
    a = _sigmoid(a0_ref[...] + lora[:, RWKV_WIDTH:2 * RWKV_WIDTH])

    _, k = lerped(OD_K, RWKV_WIDTH, OD_K)
    kx = k * kk_ref[...]
    for p in range(RWKV_PAIRS):
        ps = slice(p * LANES, (p + 1) * LANES)
        kxp = kx[:, ps]
        kkp = kxp * lax.rsqrt(_seg64_sum(kxp * kxp) + L2_EPS)
        kkn_out[:, ps] = kkp
        b_out[:, ps] = kkp * a[:, ps]
    k_out[...] = k * (1.0 + (a - 1.0) * ka_ref[...])

    _, v = lerped(OD_V, RWKV_WIDTH, OD_V)
    if has_vres:
        mix = _sigmoid(v0_ref[...] + _dot(zx_l.astype(BF16), v2_ref[...]))
        v = v + (vfirst_ref[...] - v) * mix
    v_out[...] = v

    g_out[...] = _dot(_sigmoid(gd).astype(BF16), g2_ref[...])

    cos_t = c_ref[...]
    sin_a = sa_ref[...]
    sin_b = sb_ref[...]

    def rope(t):
        return t * cos_t + pltpu.roll(t, LANES - MLA_ROPE // 2, 1) * sin_a + pltpu.roll(t, MLA_ROPE // 2, 1) * sin_b

    cq = _dot(xn, w_ref[:, OD_CQ:OD_CQ + MLA_Q_LORA])
    q_all = _dot(_rms(cq, qan_ref[...]).astype(BF16), wuq_ref[...])
    ckv = _dot(xn, w_ref[:, OD_CKV:OD_CKV + MLA_KV_LORA])
    ckv_n = _rms(ckv, kvan_ref[...]).astype(BF16)
    k_all = _dot(ckv_n, wuk_ref[...])
    vmla_out[...] = _dot(ckv_n, wuv_ref[...]).astype(vmla_out.dtype)
    lane = lax.broadcasted_iota(jnp.int32, zx.shape, 1)
    k_rope = jnp.where((lane >= MLA_NOPE) & (lane < MLA_QK), pltpu.roll(zx, MLA_NOPE, 1), 0.0)
    q_ln = qln_ref[...]
    k_ln = kln_ref[...]
    scale = MLA_QK ** -0.5
    for h in range(MLA_HEADS):
        hs = slice(h * MLA_PAD, (h + 1) * MLA_PAD)
        qh = q_all[:, hs]
        qh = qh * lax.rsqrt(jnp.sum(qh * qh, axis=-1, keepdims=True) * (1.0 / MLA_QK) + RMS_EPS) * q_ln
        q_out[:, hs] = (rope(qh) * scale).astype(q_out.dtype)
        kh = k_all[:, hs] + k_rope
        kh = kh * lax.rsqrt(jnp.sum(kh * kh, axis=-1, keepdims=True) * (1.0 / MLA_QK) + RMS_EPS) * k_ln
        kmla_out[:, hs] = rope(kh).astype(kmla_out.dtype)


def _od_in(x2, gain, w_in, mu, wa2, w0, a0, g2, k_k, k_a, qan, kvan, wuq, wuk, wuv, qln, kln,
           tables, vres, bsz, tlen, tm):
    n, d = x2.shape
    nt = tlen // tm
    row = lambda b, i: (b * nt + i, 0)
    const = lambda b, i: (0, 0)
    rs = lambda w: pl.BlockSpec((tm, w), row)
    full = lambda a: pl.BlockSpec(a.shape, const)
    params = [gain, w_in, mu, wa2, w0, a0, g2, k_k, k_a, qan, kvan, wuq, wuk, wuv, qln, kln]
    args = [x2] + params + list(tables)
    in_specs = [rs(d)] + [full(a) for a in params] + [rs(LANES)] * 3
    if vres is not None:
        v_first, v0, v2 = vres
        args += [v_first, v0, v2]
        in_specs += [rs(RWKV_WIDTH), full(v0), full(v2)]
    f32w = lambda w: jax.ShapeDtypeStruct((n, w), F32)
    bfw = lambda w: jax.ShapeDtypeStruct((n, w), BF16)
    outs = [f32w(RWKV_WIDTH)] * 7 + [bfw(MLA_HEADS * MLA_PAD), bfw(MLA_HEADS * MLA_PAD), bfw(MLA_HEADS * MLA_V)]
    out_specs = [rs(RWKV_WIDTH)] * 7 + [rs(MLA_HEADS * MLA_PAD), rs(MLA_HEADS * MLA_PAD), rs(MLA_HEADS * MLA_V)]
    carry_w = RWKV_SHIFT_COLS + LANES
    return pl.pallas_call(
        functools.partial(_od_in_kernel, has_vres=vres is not None),
        grid=(bsz, nt),
        in_specs=in_specs,
        out_specs=out_specs,
        out_shape=outs,
        scratch_shapes=[pltpu.VMEM((HALO + tm, RWKV_WIDTH), F32), pltpu.VMEM((HALO, carry_w), F32)],
        compiler_params=pltpu.CompilerParams(
            dimension_semantics=("arbitrary", "arbitrary"), vmem_limit_bytes=VMEM_LIMIT),
        name="odd_in_proj",
    )(*args)


def _rwkv_kernel(r_ref, w_ref, k_ref, v_ref, kk_ref, b_ref, g_ref, rk_ref, lnw_ref, lnb_ref,
                 o_ref, s_ref, y_ref, *, bsz, tt):
    @pl.when(pl.program_id(0) == 0)
    def _():
        s_ref[...] = jnp.zeros_like(s_ref)

    d = RWKV_DIM
    sub = lax.broadcasted_iota(jnp.int32, (d, LANES), 0)
    lane = lax.broadcasted_iota(jnp.int32, (d, LANES), 1)
    lo = lane < d
    diag_lo = lane == sub
    diag_hi = lane == sub + d
    diag = diag_lo | diag_hi
    lane_row = lax.broadcasted_iota(jnp.int32, (1, LANES), 1)
    lo_row = lane_row < d

    def split(x):
        return jnp.where(lo_row, x, 0.0), jnp.where(lo_row, 0.0, x)

    def seg_matvec(state, vec_lo, vec_hi):
        s_lo = jnp.sum(state * vec_lo, axis=-1, keepdims=True)
        s_hi = jnp.sum(state * vec_hi, axis=-1, keepdims=True)
        return jnp.where(lo, s_lo, s_hi)

    def step(t, carry):
        for bb in range(bsz):
            r_t, w_t, k_t, v_t, kk_t, b_t = [ref[bb, pl.ds(t, 1), :] for ref in (r_ref, w_ref, k_ref, v_ref, kk_ref, b_ref)]
            y_rows = []
            for p in range(RWKV_PAIRS):
                ps = slice(p * LANES, (p + 1) * LANES)
                state = s_ref[bb * RWKV_PAIRS + p]
                kk_lo, kk_hi = split(kk_t[:, ps])
                sa = -seg_matvec(state, kk_lo, kk_hi)
                v_row = jnp.broadcast_to(v_t[:, ps], (d, LANES))
                v_lo = jnp.sum(jnp.where(diag_lo, v_row, 0.0), axis=-1, keepdims=True)
                v_hi = jnp.sum(jnp.where(diag_hi, v_row, 0.0), axis=-1, keepdims=True)
                v_col = jnp.where(lo, v_lo, v_hi)
                state = state * w_t[:, ps] + sa * b_t[:, ps] + v_col * k_t[:, ps]
                s_ref[bb * RWKV_PAIRS + p] = state
                r_lo, r_hi = split(r_t[:, ps])
                y_col = seg_matvec(state, r_lo, r_hi)
                y_rows.append(jnp.sum(jnp.where(diag, y_col, 0.0), axis=0, keepdims=True))
            y_ref[bb, pl.ds(t, 1), :] = jnp.concatenate(y_rows, axis=-1)
        return carry

    lax.fori_loop(0, tt, step, 0)

    for bb in range(bsz):
        for p in range(RWKV_PAIRS):
            ps = slice(p * LANES, (p + 1) * LANES)
            y = y_ref[bb, :, ps]
            mu = _seg64_sum(y) * (1.0 / d)
            yc = y - mu
            var = _seg64_sum(yc * yc) * (1.0 / d)
            y = yc * lax.rsqrt(var + RWKV_LN_EPS) * lnw_ref[:, ps] + lnb_ref[:, ps]
            bonus = _seg64_sum(r_ref[bb, :, ps] * k_ref[bb, :, ps] * rk_ref[:, ps])
            y = (y + bonus * v_ref[bb, :, ps]) * g_ref[bb, :, ps]
            o_ref[bb, :, ps] = y.astype(o_ref.dtype)


def _rwkv(r, w, k, v, kk, b, g, rk, lnw, lnb, tt):
    bsz, tlen, width = r.shape
    blk = pl.BlockSpec((bsz, tt, width), lambda i: (0, i, 0))
    par = pl.BlockSpec((1, width), lambda i: (0, 0))
    return pl.pallas_call(
        functools.partial(_rwkv_kernel, bsz=bsz, tt=tt),
        grid=(tlen // tt,),
        in_specs=[blk] * 7 + [par] * 3,
        out_specs=blk,
        out_shape=jax.ShapeDtypeStruct((bsz, tlen, width), BF16),
        scratch_shapes=[pltpu.VMEM((bsz * RWKV_PAIRS, RWKV_DIM, LANES), F32),
                        pltpu.VMEM((bsz, tt, width), F32)],
        compiler_params=pltpu.CompilerParams(
            dimension_semantics=("arbitrary",), vmem_limit_bytes=VMEM_LIMIT),
        name="rwkv7_scan",
    )(r, w, k, v, kk, b, g, rk, lnw, lnb)


def _mla_kernel(q_ref, k_ref, v_ref, o_ref, *, tq):
    qi = pl.program_id(2)
    row = lax.broadcasted_iota(jnp.int32, (tq, tq), 0)
    col = lax.broadcasted_iota(jnp.int32, (tq, tq), 1)
    lane = lax.broadcasted_iota(jnp.int32, (tq, LANES), 1)
    outs = []
    for h in range(2):
        hs = slice(h * MLA_PAD, (h + 1) * MLA_PAD)
        q = q_ref[:, hs]

        def block(j, carry, masked):
            m, l, acc = carry
            start = pl.multiple_of(j * tq, tq)
            s = _dot_nt(q, k_ref[pl.ds(start, tq), hs])
            if masked:
                s = jnp.where(col <= row, s, -jnp.inf)
            m_new = jnp.maximum(m, jnp.max(s, axis=-1, keepdims=True))
            alpha = jnp.exp(m - m_new)
            p = jnp.exp(s - m_new)
            l = alpha * l + jnp.sum(p, axis=-1, keepdims=True)
            acc = alpha * acc + _dot(p.astype(BF16), v_ref[pl.ds(start, tq), :])
            return m_new, l, acc

        init = (jnp.full((tq, 1), -jnp.inf, F32), jnp.zeros((tq, 1), F32), jnp.zeros((tq, LANES), F32))
        carry = lax.fori_loop(0, qi, functools.partial(block, masked=False), init)
        m, l, acc = block(qi, carry, True)
        outs.append(acc / l)
    o_ref[...] = jnp.where(lane < MLA_V, outs[0], outs[1]).astype(o_ref.dtype)


def _mla(q, k, v, bsz, tlen, tq):
    n = q.shape[0]
    nq = tlen // tq
    hp = MLA_HEADS // 2
    return pl.pallas_call(
        functools.partial(_mla_kernel, tq=tq),
        grid=(bsz, hp, nq),
        in_specs=[pl.BlockSpec((tq, 2 * MLA_PAD), lambda b, h, i: (b * nq + i, h)),
                  pl.BlockSpec((tlen, 2 * MLA_PAD), lambda b, h, i: (b, h)),
                  pl.BlockSpec((tlen, 2 * MLA_V), lambda b, h, i: (b, h))],
        out_specs=pl.BlockSpec((tq, 2 * MLA_V), lambda b, h, i: (b * nq + i, h)),
        out_shape=jax.ShapeDtypeStruct((n, MLA_HEADS * MLA_V), BF16),
        compiler_params=pltpu.CompilerParams(
            dimension_semantics=("arbitrary", "arbitrary", "arbitrary"), vmem_limit_bytes=VMEM_LIMIT),
        name="mla_attention",
    )(q, k, v)


def _ffn_kernel(x_ref, ma_ref, mb_ref, woa_ref, wob_ref, gain_ref, wgu_ref, wd_ref, o_ref, h_ref, *, fc):
    d_ff = wd_ref.shape[0]
    x1 = x_ref[...] + _dot(ma_ref[...], woa_ref[...]) + _dot(mb_ref[...], wob_ref[...])
    xn = _rms(x1, gain_ref[...]).astype(BF16)
    for c in range(d_ff // fc):
        gate = _dot(xn, wgu_ref[:, c * fc:(c + 1) * fc])
        up = _dot(xn, wgu_ref[:, d_ff + c * fc:d_ff + (c + 1) * fc])
        h_ref[:, c * fc:(c + 1) * fc] = (_silu(gate) * up).astype(BF16)
    o_ref[...] = x1 + _dot(h_ref[...], wd_ref[...])


def _ffn(x2, mix_a, mix_b, wo_a, wo_b, gain, w_gu, w_down, tm, fc):
    n, d = x2.shape
    d_ff = w_down.shape[0]
    row = lambda i: (i, 0)
    const = lambda i: (0, 0)
    rs = lambda w: pl.BlockSpec((tm, w), row)
    full = lambda a: pl.BlockSpec(a.shape, const, pipeline_mode=pl.Buffered(1))
    return pl.pallas_call(
        functools.partial(_ffn_kernel, fc=fc),
        grid=(n // tm,),
        in_specs=[rs(d), rs(mix_a.shape[1]), rs(mix_b.shape[1]), full(wo_a), full(wo_b), full(gain),
                  full(w_gu), full(w_down)],
        out_specs=rs(d),
        out_shape=jax.ShapeDtypeStruct((n, d), F32),
        scratch_shapes=[pltpu.VMEM((tm, d_ff), BF16)],
        compiler_params=pltpu.CompilerParams(
            dimension_semantics=("arbitrary",), vmem_limit_bytes=VMEM_LIMIT),
        name="outproj_ffn",
    )(x2, mix_a, mix_b, wo_a, wo_b, gain, w_gu, w_down)


def _row(v):
    return v.reshape(1, -1).astype(F32)


def _pad_lanes(v, width, offset=0):
    out = jnp.zeros((1, width), F32)
    return out.at[0, offset:offset + v.shape[-1]].set(v.astype(F32))


def _pad_heads(w, heads, src, dst):
    rows = w.shape[0]
    w = w.reshape(rows, heads, src)
    return jnp.pad(w, ((0, 0), (0, 0), (0, dst - src))).reshape(rows, heads * dst)


def _tile(tlen, want):
    t = min(want, tlen)
    assert tlen % t == 0, (tlen, t)
    return t


def kernel(x, positions, norm_mix, norm_ffn, ffn_w_gu, ffn_w_down, ev_w_in, ev_conv_a, ev_conv_qkv, ev_a_log, ev_dt_bias, ev_out_norm, ev_w_out, od_w_in, od_shift_mu, od_w0, od_w2, od_a0, od_a2, od_g2, od_k_k, od_k_a, od_r_k, od_lnx_w, od_lnx_b, od_vres_w1, od_vres_mu, od_vres_v0, od_vres_v2, od_qa_norm, od_kva_norm, od_w_uq, od_w_ukv, od_q_ln, od_k_ln, od_w_out):
    bsz, tlen, d = x.shape
    n = bsz * tlen
    depth = norm_mix.shape[0]
    tm_proj = _tile(tlen, 256)
    tm_ffn = _tile(tlen, 512)
    tt_gdn = _tile(tlen, 128)
    tt_rwkv = _tile(tlen, 128)
    tq_mla = _tile(tlen, 512)
    d_ff = ffn_w_down.shape[1]
    fc = 256

    x2 = x.reshape(n, d)
    tables = _rope_tables(positions, tm_proj)
    v_first = None

    for layer in range(depth):
        gain = _row(norm_mix[layer])
        if layer % 2 == 0:
            e = layer // 2
            n_main = 3 * A_WIDTH + 4 * GDN_WIDTH
            w_main = ev_w_in[e][:, :n_main].astype(BF16)
            w_small = jnp.pad(ev_w_in[e][:, n_main:], ((0, 0), (0, LANES - 2 * GDN_HEADS))).astype(BF16)
            alog_l = _pad_lanes(ev_a_log[e], LANES, GDN_HEADS)
            dtb_l = _pad_lanes(ev_dt_bias[e], LANES, GDN_HEADS)
            ya, q, k, v, gate, bg = _ev_in(x2, gain, w_main, w_small, ev_conv_a[e].astype(F32),
                                           ev_conv_qkv[e].astype(F32), alog_l, dtb_l, bsz, tlen, tm_proj)
            nchunk = tt_gdn // GDN_CHUNK
            bgt = bg[:, :2 * SUBLANES].reshape(n // tt_gdn, nchunk, GDN_CHUNK, 2 * SUBLANES).transpose(0, 1, 3, 2)
            o = _gdn(q, k, v, bg, bgt, gate, _row(ev_out_norm[e]), bsz, tlen, tt_gdn)
            mix_a, mix_b = ya, o
            w_out = ev_w_out[e].astype(BF16)
            wo_a, wo_b = w_out[:A_WIDTH], w_out[A_WIDTH:]
        else:
            o = layer // 2
            has_vres = o > 0
            w_in = od_w_in[o]
            extra = jnp.zeros((d, LANES), F32).at[:, :MLA_ROPE].set(w_in[:, RWKV_SHIFT_COLS + MLA_Q_LORA + MLA_KV_LORA:])
            mu = jnp.zeros((1, OD_COLS), F32).at[0, :RWKV_SHIFT_COLS].set(od_shift_mu[o])
            if has_vres:
                extra = extra.at[:, MLA_ROPE:MLA_ROPE + RWKV_V_LORA].set(od_vres_w1[o - 1])
                mu = mu.at[0, OD_X + MLA_ROPE:OD_X + MLA_ROPE + RWKV_V_LORA].set(od_vres_mu[o - 1])
            n_mla = MLA_Q_LORA + MLA_KV_LORA
            w_packed = jnp.concatenate(
                [w_in[:, :RWKV_SHIFT_COLS], extra, w_in[:, RWKV_SHIFT_COLS:RWKV_SHIFT_COLS + n_mla]], axis=1).astype(BF16)
            wa2 = jnp.zeros((LANES, 2 * RWKV_WIDTH), F32)
            wa2 = wa2.at[:RWKV_W_LORA, :RWKV_WIDTH].set(od_w2[o]).at[RWKV_W_LORA:, RWKV_WIDTH:].set(od_a2[o]).astype(BF16)
            wuq = _pad_heads(od_w_uq[o], MLA_HEADS, MLA_QK, MLA_PAD).astype(BF16)
            w_ukv = od_w_ukv[o].reshape(MLA_KV_LORA, MLA_HEADS, MLA_NOPE + MLA_V)
            wuk = _pad_heads(w_ukv[:, :, :MLA_NOPE].reshape(MLA_KV_LORA, -1), MLA_HEADS, MLA_NOPE, MLA_PAD).astype(BF16)
            wuv = w_ukv[:, :, MLA_NOPE:].reshape(MLA_KV_LORA, -1).astype(BF16)
            vres = None
            if has_vres:
                v2 = jnp.zeros((LANES, RWKV_WIDTH), F32).at[MLA_ROPE:MLA_ROPE + RWKV_V_LORA].set(od_vres_v2[o - 1])
                vres = (v_first, _row(od_vres_v0[o - 1]), v2.astype(BF16))
            (r, w, k, v, kk, b, g, q_mla, k_mla, v_mla) = _od_in(
                x2, gain, w_packed, mu, wa2, _row(od_w0[o]), _row(od_a0[o]), od_g2[o].astype(BF16),
                _row(od_k_k[o]), _row(od_k_a[o]), _row(od_qa_norm[o]), _row(od_kva_norm[o]), wuq, wuk, wuv,
                _pad_lanes(od_q_ln[o], MLA_PAD), _pad_lanes(od_k_ln[o], MLA_PAD), tables, vres, bsz, tlen, tm_proj)
            if not has_vres:
                v_first = v
            b3 = lambda a: a.reshape(bsz, tlen, RWKV_WIDTH)
            y_rwkv = _rwkv(b3(r), b3(w), b3(k), b3(v), b3(kk), b3(b), b3(g), _row(od_r_k[o]),
                           _row(od_lnx_w[o]), _row(od_lnx_b[o]), tt_rwkv).reshape(n, RWKV_WIDTH)
            o_mla = _mla(q_mla, k_mla, v_mla, bsz, tlen, tq_mla)
            mix_a, mix_b = y_rwkv, o_mla
            w_out = od_w_out[o].astype(BF16)
            wo_a, wo_b = w_out[:RWKV_WIDTH], w_out[RWKV_WIDTH:]
        x2 = _ffn(x2, mix_a, mix_b, wo_a, wo_b, _row(norm_ffn[layer]), ffn_w_gu[layer].astype(BF16),
                  ffn_w_down[layer].astype(BF16), tm_ffn, fc)
    return x2.reshape(bsz, tlen, d)
```

```python
import functools

import jax
import jax.numpy as jnp
from jax import lax
from jax.experimental import pallas as pl
from jax.experimental.pallas import tpu as pltpu

F32 = jnp.float32
BF16 = jnp.bfloat16
HIGHEST = lax.Precision.HIGHEST

LANES = 128
SUBLANES = 8
VMEM_LIMIT = 56 * 1024 * 1024

RMS_EPS = 1e-6
L2_EPS = 1e-6
RWKV_LN_EPS = 64e-5
ROPE_THETA = 10000.0

A_WIDTH = 256
GDN_HEADS = 6
GDN_DIM = 128
GDN_WIDTH = GDN_HEADS * GDN_DIM
GDN_CHUNK = 64
GDN_BLOCK = 16

RWKV_HEADS = 8
RWKV_DIM = 64
RWKV_WIDTH = RWKV_HEADS * RWKV_DIM
RWKV_PAIRS = RWKV_WIDTH // LANES
RWKV_CHUNK = 64
RWKV_W_LORA = 64
RWKV_A_LORA = 64
RWKV_V_LORA = 32
RWKV_G_LORA = 128
RWKV_SHIFT_COLS = 3 * RWKV_WIDTH + RWKV_W_LORA + RWKV_A_LORA + RWKV_G_LORA

MLA_HEADS = 8
MLA_NOPE = 64
MLA_ROPE = 32
MLA_V = 64
MLA_QK = MLA_NOPE + MLA_ROPE
MLA_Q_LORA = 512
MLA_KV_LORA = 256
MLA_PAD = 128

HALO = SUBLANES


def _dot(a, b, precision=None):
    return jnp.dot(a, b, preferred_element_type=F32, precision=precision)


def _dot_nt(a, b, precision=None):
    return lax.dot_general(a, b, (((1,), (1,)), ((), ())), preferred_element_type=F32, precision=precision)


def _dot_tn(a, b, precision=None):
    return lax.dot_general(a, b, (((0,), (0,)), ((), ())), preferred_element_type=F32, precision=precision)


def _rms(x, gain):
    return x * lax.rsqrt(jnp.mean(x * x, axis=-1, keepdims=True) + RMS_EPS) * gain


def _sigmoid(x):
    return 1.0 / (1.0 + jnp.exp(-x))


def _silu(x):
    return x * _sigmoid(x)


def _softplus(x):
    return jnp.maximum(x, 0.0) + jnp.log(1.0 + jnp.exp(-jnp.abs(x)))


def _seg64_sum(x):
    lo = lax.broadcasted_iota(jnp.int32, x.shape, 1) < RWKV_DIM
    s_lo = jnp.sum(jnp.where(lo, x, 0.0), axis=-1, keepdims=True)
    s_hi = jnp.sum(jnp.where(lo, 0.0, x), axis=-1, keepdims=True)
    return jnp.where(lo, s_lo, s_hi)


def _shifted_rows(z, scr_ref, carry_ref, first, shifts):
    tm, width = z.shape
    assert width >= 2 * LANES, width
    prev = carry_ref[...]
    scr_ref[0:HALO, :] = jnp.where(first, jnp.zeros_like(prev), prev)
    scr_ref[HALO:HALO + tm, :] = z
    carry_ref[...] = z[tm - HALO:, :]
    return [scr_ref[pl.ds(HALO - s, tm), :] for s in shifts]


def _rope_kernel(pos_ref, invf_ref, c_ref, sa_ref, sb_ref):
    ang = pos_ref[...] * invf_ref[...]
    lane = lax.broadcasted_iota(jnp.int32, ang.shape, 1)
    cos = jnp.cos(ang)
    sin = jnp.sin(ang)
    half = MLA_ROPE // 2
    c_ref[...] = jnp.where(lane < MLA_NOPE, 1.0, jnp.where(lane < MLA_QK, cos, 0.0))
    sa_ref[...] = jnp.where((lane >= MLA_NOPE) & (lane < MLA_NOPE + half), -sin, 0.0)
    sb_ref[...] = jnp.where((lane >= MLA_NOPE + half) & (lane < MLA_QK), sin, 0.0)


def _rope_tables(positions, tm):
    n = positions.size
    pos = jnp.broadcast_to(positions.reshape(n, 1).astype(F32), (n, LANES))
    inv_freq = ROPE_THETA ** (-jnp.arange(0, MLA_ROPE, 2, dtype=F32) / MLA_ROPE)
    half = MLA_ROPE // 2
    invf = jnp.zeros((1, LANES), F32)
    invf = invf.at[0, MLA_NOPE:MLA_NOPE + half].set(inv_freq).at[0, MLA_NOPE + half:MLA_QK].set(inv_freq)
    spec = pl.BlockSpec((tm, LANES), lambda i: (i, 0))
    out = jax.ShapeDtypeStruct((n, LANES), F32)
    return pl.pallas_call(
        _rope_kernel,
        grid=(n // tm,),
        in_specs=[spec, pl.BlockSpec((1, LANES), lambda i: (0, 0))],
        out_specs=[spec, spec, spec],
        out_shape=[out, out, out],
        name="rope_tables",
    )(pos, invf)


def _ev_in_kernel(x_ref, gain_ref, w_ref, ws_ref, ca_ref, cq_ref, alog_ref, dtb_ref,
                  ya_ref, q_ref, k_ref, v_ref, gate_ref, bg_ref,
                  scr_ref, carry_p_ref, carry_qkv_ref):
    first = pl.program_id(1) == 0
    xn = _rms(x_ref[...], gain_ref[...]).astype(BF16)

    za = _dot(xn, w_ref[:, 0:3 * A_WIDTH])
    a_b = za[:, 0:A_WIDTH]
    prod = za[:, A_WIDTH:2 * A_WIDTH] * za[:, 2 * A_WIDTH:3 * A_WIDTH]
    p1, p2 = _shifted_rows(prod, scr_ref.at[:, 0:A_WIDTH], carry_p_ref, first, (1, 2))
    ca = ca_ref[...]
    ya = a_b * (ca[0:1, :] * p2 + ca[1:2, :] * p1 + ca[2:3, :] * prod)
    ya_ref[...] = ya.astype(ya_ref.dtype)

    base = 3 * A_WIDTH
    for g, out_ref in enumerate((q_ref, k_ref, v_ref)):
        cols = slice(base + g * GDN_WIDTH, base + (g + 1) * GDN_WIDTH)
        z = _dot(xn, w_ref[:, cols])
        z1, z2, z3 = _shifted_rows(z, scr_ref, carry_qkv_ref.at[g], first, (1, 2, 3))
        cw = cq_ref[:, g * GDN_WIDTH:(g + 1) * GDN_WIDTH]
        y = _silu(cw[0:1, :] * z3 + cw[1:2, :] * z2 + cw[2:3, :] * z1 + cw[3:4, :] * z)
        for h in range(GDN_HEADS):
            hs = slice(h * GDN_DIM, (h + 1) * GDN_DIM)
            yh = y[:, hs]
            if g < 2:
                yh = yh * lax.rsqrt(jnp.sum(yh * yh, axis=-1, keepdims=True) + L2_EPS)
            if g == 0:
                yh = yh * (GDN_DIM ** -0.5)
            out_ref[:, hs] = yh

    gcols = slice(base + 3 * GDN_WIDTH, base + 4 * GDN_WIDTH)
    gate_ref[...] = _dot(xn, w_ref[:, gcols])

    zs = _dot(xn, ws_ref[...])
    lane = lax.broadcasted_iota(jnp.int32, zs.shape, 1)
    g_log = -jnp.exp(alog_ref[...]) * _softplus(zs + dtb_ref[...])
    bg_ref[...] = jnp.where(lane < GDN_HEADS, _sigmoid(zs), g_log)


def _ev_in(x2, gain, w_main, w_small, conv_a, conv_qkv, alog_l, dtb_l, bsz, tlen, tm):
    n, d = x2.shape
    nt = tlen // tm
    row = lambda b, i: (b * nt + i, 0)
    const = lambda b, i: (0, 0)
    rs = lambda w: pl.BlockSpec((tm, w), row)
    full = lambda a: pl.BlockSpec(a.shape, const)
    outs = [
        jax.ShapeDtypeStruct((n, A_WIDTH), BF16),
        jax.ShapeDtypeStruct((n, GDN_WIDTH), F32),
        jax.ShapeDtypeStruct((n, GDN_WIDTH), F32),
        jax.ShapeDtypeStruct((n, GDN_WIDTH), F32),
        jax.ShapeDtypeStruct((n, GDN_WIDTH), F32),
        jax.ShapeDtypeStruct((n, LANES), F32),
    ]
    return pl.pallas_call(
        _ev_in_kernel,
        grid=(bsz, nt),
        in_specs=[rs(d), full(gain), full(w_main), full(w_small), full(conv_a), full(conv_qkv),
                  full(alog_l), full(dtb_l)],
        out_specs=[rs(A_WIDTH), rs(GDN_WIDTH), rs(GDN_WIDTH), rs(GDN_WIDTH), rs(GDN_WIDTH), rs(LANES)],
        out_shape=outs,
        scratch_shapes=[
            pltpu.VMEM((HALO + tm, GDN_WIDTH), F32),
            pltpu.VMEM((HALO, A_WIDTH), F32),
            pltpu.VMEM((3, HALO, GDN_WIDTH), F32),
        ],
        compiler_params=pltpu.CompilerParams(
            dimension_semantics=("arbitrary", "arbitrary"), vmem_limit_bytes=VMEM_LIMIT),
        name="even_in_proj",
    )(x2, gain, w_main, w_small, conv_a, conv_qkv, alog_l, dtb_l)


def _unit_lower_inverse_many(l_mats, row, col, dot):
    eye = (row == col).astype(F32)
    in_block = (row // GDN_BLOCK) == (col // GDN_BLOCK)
    l_diag = [jnp.where(in_block, m, 0.0) for m in l_mats]
    l_off = [jnp.where(in_block, 0.0, m) for m in l_mats]
    x2 = [dot(m, m) for m in l_diag]
    x4 = [dot(m, m) for m in x2]
    x8 = [dot(m, m) for m in x4]
    d_inv = [eye - m for m in l_diag]
    d_inv = [m + dot(m, x) for m, x in zip(d_inv, x2)]
    d_inv = [m + dot(m, x) for m, x in zip(d_inv, x4)]
    d_inv = [m + dot(m, x) for m, x in zip(d_inv, x8)]
    n1 = [dot(m, x) for m, x in zip(d_inv, l_off)]
    n2 = [dot(m, m) for m in n1]
    outer = [eye - m for m in n1]
    outer = [m + dot(m, x) for m, x in zip(outer, n2)]
    return [dot(m, x) for m, x in zip(outer, d_inv)]


def _gdn_kernel(q_ref, k_ref, v_ref, bg_ref, bgt_ref, gate_ref, onorm_ref, o_ref, s_ref, *, nchunk):
    @pl.when(pl.program_id(1) == 0)
    def _():
        s_ref[...] = jnp.zeros_like(s_ref)

    c = GDN_CHUNK
    dk = GDN_DIM
    pairs = range(GDN_HEADS // 2)
    row_c = lax.broadcasted_iota(jnp.int32, (c, c), 0)
    col_c = lax.broadcasted_iota(jnp.int32, (c, c), 1)
    tri = (row_c >= col_c).astype(F32)
    row1 = lax.broadcasted_iota(jnp.int32, (2 * c, 2 * c), 0)
    col1 = lax.broadcasted_iota(jnp.int32, (2 * c, 2 * c), 1)
    same_head = (row1 // c) == (col1 // c)
    lag = (row1 % c) - (col1 % c)
    causal = same_head & (lag >= 0)
    strict = same_head & (lag > 0)
    tri_t = (same_head & (lag <= 0)).astype(F32)
    head0_rows = (lax.broadcasted_iota(jnp.int32, (2 * c, 1), 0) < c).astype(F32)
    onorm = onorm_ref[...]

    def bdot(a, b):
        return _dot(a.astype(BF16), b.astype(BF16))

    def rows2(a, b):
        return jnp.concatenate([a, b], axis=0)

    for ci in range(nchunk):
        rows = slice(ci * c, (ci + 1) * c)
        bg = bg_ref[rows, :]
        g_cols = _dot(tri, bg, precision=HIGHEST)
        g_rows = _dot(bgt_ref[ci], tri_t, precision=HIGHEST)
        q_s, k_s, v_s, beta_s, gc_s, gl_s, l_mats, qk = [], [], [], [], [], [], [], []
        for p in pairs:
            h0, h1 = 2 * p, 2 * p + 1
            s0, s1 = slice(h0 * dk, (h0 + 1) * dk), slice(h1 * dk, (h1 + 1) * dk)
            q_s.append(rows2(q_ref[rows, s0], q_ref[rows, s1]))
            k_s.append(rows2(k_ref[rows, s0], k_ref[rows, s1]))
            v_s.append(rows2(v_ref[rows, s0], v_ref[rows, s1]))
            beta_s.append(rows2(bg[:, h0:h0 + 1], bg[:, h1:h1 + 1]))
            gc0 = g_cols[:, GDN_HEADS + h0:GDN_HEADS + h0 + 1]
            gc1 = g_cols[:, GDN_HEADS + h1:GDN_HEADS + h1 + 1]
            gc_s.append(rows2(gc0, gc1))
            gl_s.append(rows2(jnp.broadcast_to(gc0[c - 1:c, :], (c, 1)), jnp.broadcast_to(gc1[c - 1:c, :], (c, 1))))
            diff = gc_s[p] - g_rows[p:p + 1, :]
            decay = jnp.where(causal, jnp.exp(jnp.where(causal, diff, 0.0)), 0.0)
            kb = k_s[p] * beta_s[p]
            score = _dot_nt(rows2(kb, q_s[p]).astype(BF16), k_s[p].astype(BF16))
            l_mats.append(jnp.where(strict, score[0:2 * c] * decay, 0.0))
            qk.append(score[2 * c:4 * c] * decay)
        t_inv = _unit_lower_inverse_many(l_mats, row1, col1, bdot)
        for p in pairs:
            eg = jnp.exp(gc_s[p])
            wu = bdot(t_inv[p], jnp.concatenate([k_s[p] * (beta_s[p] * eg), v_s[p] * beta_s[p]], axis=1))
            qy = bdot(qk[p], wu)
            q_hat = q_s[p] * eg - qy[:, 0:dk]
            y0 = qy[:, dk:]
            k_dec = k_s[p] * jnp.exp(gl_s[p] - gc_s[p])
            wu0 = wu * head0_rows
            wu1 = wu - wu0
            expanded = jnp.concatenate([wu0[:, 0:dk], wu1[:, 0:dk], wu0[:, dk:], wu1[:, dk:]], axis=1)
            gd = _dot_tn(k_dec.astype(BF16), expanded.astype(BF16))
            for hh in range(2):
                h = 2 * p + hh
                hs = slice(h * dk, (h + 1) * dk)
                hr = slice(hh * c, (hh + 1) * c)
                state = s_ref[h]
                o = bdot(q_hat[hr], state) + y0[hr]
                g_last = gl_s[p][hh * c:hh * c + 1, :]
                s_ref[h] = state * jnp.exp(g_last) - bdot(gd[:, hh * dk:(hh + 1) * dk], state) + gd[:, (2 + hh) * dk:(3 + hh) * dk]
                o = _rms(o, onorm) * _silu(gate_ref[rows, hs])
                o_ref[rows, hs] = o.astype(o_ref.dtype)


def _gdn(q, k, v, bg, bgt, gate, onorm, bsz, tlen, tt):
    n = q.shape[0]
    nt = tlen // tt
    nchunk = tt // GDN_CHUNK
    row = lambda b, i: (b * nt + i, 0)
    rs = lambda w: pl.BlockSpec((tt, w), row)
    return pl.pallas_call(
        functools.partial(_gdn_kernel, nchunk=nchunk),
        grid=(bsz, nt),
        in_specs=[rs(GDN_WIDTH), rs(GDN_WIDTH), rs(GDN_WIDTH), rs(LANES),
                  pl.BlockSpec((nchunk, SUBLANES, 2 * GDN_CHUNK), lambda b, i: (b * nt + i, 0, 0)),
                  rs(GDN_WIDTH), pl.BlockSpec(onorm.shape, lambda b, i: (0, 0))],
        out_specs=rs(GDN_WIDTH),
        out_shape=jax.ShapeDtypeStruct((n, GDN_WIDTH), BF16),
        scratch_shapes=[pltpu.VMEM((GDN_HEADS, GDN_DIM, GDN_DIM), F32)],
        compiler_params=pltpu.CompilerParams(
            dimension_semantics=("arbitrary", "arbitrary"), vmem_limit_bytes=VMEM_LIMIT),
        name="gated_delta_rule",
    )(q, k, v, bg, bgt, gate, onorm)


OD_R = 0
OD_K = RWKV_WIDTH
OD_V = 2 * RWKV_WIDTH
OD_WA = 3 * RWKV_WIDTH
OD_G = OD_WA + LANES
OD_X = RWKV_SHIFT_COLS
OD_SMALL = 3 * LANES
OD_CQ = OD_X + LANES
OD_CKV = OD_CQ + MLA_Q_LORA
OD_COLS = OD_CKV + MLA_KV_LORA


def _od_in_kernel(*refs, has_vres):
    (x_ref, gain_ref, w_ref, mu_ref, wa2_ref, w0_ref, a0_ref, g2_ref, kk_ref, ka_ref,
     qan_ref, kvan_ref, wuq_ref, wuk_ref, wuv_ref, qln_ref, kln_ref,
     c_ref, sa_ref, sb_ref) = refs[:20]
    pos = 20
    if has_vres:
        vfirst_ref, v0_ref, v2_ref = refs[pos:pos + 3]
        pos += 3
    (r_out, lw_out, k_out, v_out, kkn_out, b_out, g_out, q_out, kmla_out, vmla_out) = refs[pos:pos + 10]
    scr_ref, carry_ref = refs[pos + 10:]

    first = pl.program_id(1) == 0
    xn = _rms(x_ref[...], gain_ref[...]).astype(BF16)

    def lerped(start, width, slot):
        z = _dot(xn, w_ref[:, start:start + width])
        (zp,) = _shifted_rows(z, scr_ref.at[:, 0:width], carry_ref.at[:, slot:slot + width], first, (1,))
        return z, z + mu_ref[:, start:start + width] * (zp - z)

    _, r = lerped(OD_R, RWKV_WIDTH, OD_R)
    r_out[...] = r

    small, small_l = lerped(OD_WA, OD_SMALL, OD_WA)
    wa = small_l[:, 0:LANES]
    gd = small_l[:, LANES:2 * LANES]
    zx = small[:, 2 * LANES:3 * LANES]
    zx_l = small_l[:, 2 * LANES:3 * LANES]
    lane = lax.broadcasted_iota(jnp.int32, wa.shape, 1)
    lora_in = jnp.where(lane < RWKV_W_LORA, jnp.tanh(wa), wa).astype(BF16)
    lora = _dot(lora_in, wa2_ref[...])
    w_log = -_softplus(-(w0_ref[...] + lora[:, 0:RWKV_WIDTH])) - 0.5
    lw_out[...] = -jnp.exp(w_log)
    a = _sigmoid(a0_ref[...] + lora[:, RWKV_WIDTH:2 * RWKV_WIDTH])

    _, k = lerped(OD_K, RWKV_WIDTH, OD_K)
    kx = k * kk_ref[...]
    for p in range(RWKV_PAIRS):
        ps = slice(p * LANES, (p + 1) * LANES)
        kxp = kx[:, ps]
        kkp = kxp * lax.rsqrt(_seg64_sum(kxp * kxp) + L2_EPS)
        kkn_out[:, ps] = kkp
        b_out[:, ps] = kkp * a[:, ps]
    k_out[...] = k * (1.0 + (a - 1.0) * ka_ref[...])

    _, v = lerped(OD_V, RWKV_WIDTH, OD_V)
    if has_vres:
        mix = _sigmoid(v0_ref[...] + _dot(zx_l.astype(BF16), v2_ref[...]))
        v = v + (vfirst_ref[...] - v) * mix
    v_out[...] = v

    g_out[...] = _dot(_sigmoid(gd).astype(BF16), g2_ref[...])

    cos_t = c_ref[...]
    sin_a = sa_ref[...]
    sin_b = sb_ref[...]

    def rope(t):
        return t * cos_t + pltpu.roll(t, LANES - MLA_ROPE // 2, 1) * sin_a + pltpu.roll(t, MLA_ROPE // 2, 1) * sin_b

    cq = _dot(xn, w_ref[:, OD_CQ:OD_CQ + MLA_Q_LORA])
    q_all = _dot(_rms(cq, qan_ref[...]).astype(BF16), wuq_ref[...])
    ckv = _dot(xn, w_ref[:, OD_CKV:OD_CKV + MLA_KV_LORA])
    ckv_n = _rms(ckv, kvan_ref[...]).astype(BF16)
    k_all = _dot(ckv_n, wuk_ref[...])
    vmla_out[...] = _dot(ckv_n, wuv_ref[...]).astype(vmla_out.dtype)
    lane = lax.broadcasted_iota(jnp.int32, zx.shape, 1)
    k_rope = jnp.where((lane >= MLA_NOPE) & (lane < MLA_QK), pltpu.roll(zx, MLA_NOPE, 1), 0.0)
    q_ln = qln_ref[...]
    k_ln = kln_ref[...]
    scale = MLA_QK ** -0.5
    for h in range(MLA_HEADS):
        hs = slice(h * MLA_PAD, (h + 1) * MLA_PAD)
        qh = q_all[:, hs]
        qh = qh * lax.rsqrt(jnp.sum(qh * qh, axis=-1, keepdims=True) * (1.0 / MLA_QK) + RMS_EPS) * q_ln
        q_out[:, hs] = (rope(qh) * scale).astype(q_out.dtype)
        kh = k_all[:, hs] + k_rope
        kh = kh * lax.rsqrt(jnp.sum(kh * kh, axis=-1, keepdims=True) * (1.0 / MLA_QK) + RMS_EPS) * k_ln
        kmla_out[:, hs] = rope(kh).astype(kmla_out.dtype)


def _od_in(x2, gain, w_in, mu, wa2, w0, a0, g2, k_k, k_a, qan, kvan, wuq, wuk, wuv, qln, kln,
           tables, vres, bsz, tlen, tm):
    n, d = x2.shape
    nt = tlen // tm
    row = lambda b, i: (b * nt + i, 0)
    const = lambda b, i: (0, 0)
    rs = lambda w: pl.BlockSpec((tm, w), row)
    full = lambda a: pl.BlockSpec(a.shape, const)
    params = [gain, w_in, mu, wa2, w0, a0, g2, k_k, k_a, qan, kvan, wuq, wuk, wuv, qln, kln]
    args = [x2] + params + list(tables)
    in_specs = [rs(d)] + [full(a) for a in params] + [rs(LANES)] * 3
    if vres is not None:
        v_first, v0, v2 = vres
        args += [v_first, v0, v2]
        in_specs += [rs(RWKV_WIDTH), full(v0), full(v2)]
    f32w = lambda w: jax.ShapeDtypeStruct((n, w), F32)
    bfw = lambda w: jax.ShapeDtypeStruct((n, w), BF16)
    outs = [f32w(RWKV_WIDTH)] * 7 + [bfw(MLA_HEADS * MLA_PAD), bfw(MLA_HEADS * MLA_PAD), bfw(MLA_HEADS * MLA_V)]
    out_specs = [rs(RWKV_WIDTH)] * 7 + [rs(MLA_HEADS * MLA_PAD), rs(MLA_HEADS * MLA_PAD), rs(MLA_HEADS * MLA_V)]
    carry_w = RWKV_SHIFT_COLS + LANES
    return pl.pallas_call(
        functools.partial(_od_in_kernel, has_vres=vres is not None),
        grid=(bsz, nt),
        in_specs=in_specs,
        out_specs=out_specs,
        out_shape=outs,
        scratch_shapes=[pltpu.VMEM((HALO + tm, RWKV_WIDTH), F32), pltpu.VMEM((HALO, carry_w), F32)],
        compiler_params=pltpu.CompilerParams(
            dimension_semantics=("arbitrary", "arbitrary"), vmem_limit_bytes=VMEM_LIMIT),
        name="odd_in_proj",
    )(*args)


def _rwkv_kernel(r_ref, lw_ref, k_ref, v_ref, kk_ref, b_ref, g_ref, rk_ref, lnw_ref, lnb_ref,
                 o_ref, h_ref, y_ref, *, nchunk):
    @pl.when(pl.program_id(1) == 0)
    def _():
        h_ref[...] = jnp.zeros_like(h_ref)

    c = RWKV_CHUNK
    d = RWKV_DIM
    lane = lax.broadcasted_iota(jnp.int32, (1, LANES), 1)
    m_lo = (lane < d).astype(F32)
    m_hi = 1.0 - m_lo
    row_c = lax.broadcasted_iota(jnp.int32, (c, c), 0)
    col_c = lax.broadcasted_iota(jnp.int32, (c, c), 1)
    tri = (row_c >= col_c).astype(F32)
    row1 = lax.broadcasted_iota(jnp.int32, (2 * c, 2 * c), 0)
    col1 = lax.broadcasted_iota(jnp.int32, (2 * c, 2 * c), 1)
    eye1 = row1 == col1
    row2 = lax.broadcasted_iota(jnp.int32, (4 * c, 4 * c), 0)
    col2 = lax.broadcasted_iota(jnp.int32, (4 * c, 4 * c), 1)
    same_head = ((row2 // c) % 2) == ((col2 // c) % 2)
    min_lag = jnp.where(row2 < 2 * c, 1, 0)
    keep = same_head & ((row2 % c) - (col2 % c) >= min_lag)

    def bdot(a, b):
        return _dot(a.astype(BF16), b.astype(BF16))

    def stack(x):
        return jnp.concatenate([x * m_lo, x * m_hi], axis=0)

    pairs = range(RWKV_PAIRS)
    for ci in range(nchunk):
        rows = slice(ci * c, (ci + 1) * c)
        lw = lw_ref[rows, :]
        logw = _dot(tri, lw, precision=HIGHEST)
        logw_last = logw[c - 1:c, :]
        w_cum = jnp.exp(logw)
        w_inv = jnp.exp(-logw)
        w_last = jnp.exp(logw_last)
        to_end = jnp.exp(logw_last - logw)
        am = kk_ref[rows, :] * jnp.exp(logw - lw)
        qt = r_ref[rows, :] * w_cum
        bp = b_ref[rows, :] * w_inv
        kp = k_ref[rows, :] * w_inv
        bh = b_ref[rows, :] * to_end
        kh = k_ref[rows, :] * to_end
        v = v_ref[rows, :]
        ps = [slice(p * LANES, (p + 1) * LANES) for p in pairs]
        am_s = [stack(am[:, s]) for s in ps]
        qt_s = [stack(qt[:, s]) for s in ps]
        v_s = [stack(v[:, s]) for s in ps]
        lhs1 = [jnp.concatenate([am_s[p], qt_s[p]], axis=0).astype(BF16) for p in pairs]
        rhs1 = [jnp.concatenate([bp[:, s], bp[:, s], kp[:, s], kp[:, s]], axis=0).astype(BF16) for s in ps]
        score = [jnp.where(keep, _dot_nt(lhs1[p], rhs1[p]), 0.0) for p in pairs]
        t_inv = _unit_lower_inverse_many([sc[0:2 * c, 0:2 * c] for sc in score], row1, col1, bdot)
        r1 = [bdot(score[p][0:2 * c, 2 * c:4 * c], v_s[p]) for p in pairs]
        x = [bdot(t_inv[p], jnp.concatenate([am_s[p], r1[p]], axis=1)) for p in pairs]
        zero = jnp.zeros((2 * c, LANES), F32)
        rhs2 = [jnp.concatenate([jnp.concatenate([x[p][:, 0:LANES], -x[p][:, LANES:]], axis=1),
                                 jnp.concatenate([zero, v_s[p]], axis=1)], axis=0).astype(BF16) for p in pairs]
        qy = [_dot(score[p][2 * c:4 * c, :].astype(BF16), rhs2[p]) for p in pairs]
        lhs_t = [jnp.concatenate([stack(bh[:, s]), stack(kh[:, s])], axis=0).astype(BF16) for s in ps]
        gd = [_dot_tn(lhs_t[p], rhs2[p]) for p in pairs]
        for p in pairs:
            q_hat = qt[:, ps[p]] - (qy[p][0:c, 0:LANES] + qy[p][c:2 * c, 0:LANES])
            y0 = qy[p][0:c, LANES:] + qy[p][c:2 * c, LANES:]
            h = h_ref[p]
            y_ref[rows, ps[p]] = bdot(q_hat, h) + y0
            wl_col = jnp.sum(jnp.where(eye1, w_last[:, ps[p]], 0.0), axis=1, keepdims=True)
            h_ref[p] = wl_col * h - bdot(gd[p][:, 0:LANES], h) + gd[p][:, LANES:]

    for p in pairs:
        s = slice(p * LANES, (p + 1) * LANES)
        y = y_ref[:, s]
        mu = _seg64_sum(y) * (1.0 / d)
        yc = y - mu
        var = _seg64_sum(yc * yc) * (1.0 / d)
        y = yc * lax.rsqrt(var + RWKV_LN_EPS) * lnw_ref[:, s] + lnb_ref[:, s]
        bonus = _seg64_sum(r_ref[:, s] * k_ref[:, s] * rk_ref[:, s])
        y = (y + bonus * v_ref[:, s]) * g_ref[:, s]
        o_ref[:, s] = y.astype(o_ref.dtype)


def _rwkv(r, lw, k, v, kk, b, g, rk, lnw, lnb, bsz, tlen, tt):
    n, width = r.shape
    nt = tlen // tt
    blk = pl.BlockSpec((tt, width), lambda bb, i: (bb * nt + i, 0))
    par = pl.BlockSpec((1, width), lambda bb, i: (0, 0))
    return pl.pallas_call(
        functools.partial(_rwkv_kernel, nchunk=tt // RWKV_CHUNK),
        grid=(bsz, nt),
        in_specs=[blk] * 7 + [par] * 3,
        out_specs=blk,
        out_shape=jax.ShapeDtypeStruct((n, width), BF16),
        scratch_shapes=[pltpu.VMEM((RWKV_PAIRS, LANES, LANES), F32), pltpu.VMEM((tt, width), F32)],
        compiler_params=pltpu.CompilerParams(
            dimension_semantics=("arbitrary", "arbitrary"), vmem_limit_bytes=VMEM_LIMIT),
        name="rwkv7_chunked",
    )(r, lw, k, v, kk, b, g, rk, lnw, lnb)


def _mla_kernel(q_ref, k_ref, v_ref, o_ref, *, tq):
    qi = pl.program_id(2)
    row = lax.broadcasted_iota(jnp.int32, (tq, tq), 0)
    col = lax.broadcasted_iota(jnp.int32, (tq, tq), 1)
    lane = lax.broadcasted_iota(jnp.int32, (tq, LANES), 1)
    outs = []
    for h in range(2):
        hs = slice(h * MLA_PAD, (h + 1) * MLA_PAD)
        q = q_ref[:, hs]

        def block(j, carry, masked):
            m, l, acc = carry
            start = pl.multiple_of(j * tq, tq)
            s = _dot_nt(q, k_ref[pl.ds(start, tq), hs])
            if masked:
                s = jnp.where(col <= row, s, -jnp.inf)
            m_new = jnp.maximum(m, jnp.max(s, axis=-1, keepdims=True))
            alpha = jnp.exp(m - m_new)
            p = jnp.exp(s - m_new)
            l = alpha * l + jnp.sum(p, axis=-1, keepdims=True)
            acc = alpha * acc + _dot(p.astype(BF16), v_ref[pl.ds(start, tq), :])
            return m_new, l, acc

        init = (jnp.full((tq, 1), -jnp.inf, F32), jnp.zeros((tq, 1), F32), jnp.zeros((tq, LANES), F32))
        carry = lax.fori_loop(0, qi, functools.partial(block, masked=False), init)
        m, l, acc = block(qi, carry, True)
        outs.append(acc / l)
    o_ref[...] = jnp.where(lane < MLA_V, outs[0], outs[1]).astype(o_ref.dtype)


def _mla(q, k, v, bsz, tlen, tq):
    n = q.shape[0]
    nq = tlen // tq
    hp = MLA_HEADS // 2
    return pl.pallas_call(
        functools.partial(_mla_kernel, tq=tq),
        grid=(bsz, hp, nq),
        in_specs=[pl.BlockSpec((tq, 2 * MLA_PAD), lambda b, h, i: (b * nq + i, h)),
                  pl.BlockSpec((tlen, 2 * MLA_PAD), lambda b, h, i: (b, h)),
                  pl.BlockSpec((tlen, 2 * MLA_V), lambda b, h, i: (b, h))],
        out_specs=pl.BlockSpec((tq, 2 * MLA_V), lambda b, h, i: (b * nq + i, h)),
        out_shape=jax.ShapeDtypeStruct((n, MLA_HEADS * MLA_V), BF16),
        compiler_params=pltpu.CompilerParams(
            dimension_semantics=("arbitrary", "arbitrary", "arbitrary"), vmem_limit_bytes=VMEM_LIMIT),
        name="mla_attention",
    )(q, k, v)


def _ffn_kernel(x_ref, ma_ref, mb_ref, woa_ref, wob_ref, gain_ref, wgu_ref, wd_ref, o_ref, h_ref, *, fc):
    d_ff = wd_ref.shape[0]
    x1 = x_ref[...] + _dot(ma_ref[...], woa_ref[...]) + _dot(mb_ref[...], wob_ref[...])
    xn = _rms(x1, gain_ref[...]).astype(BF16)
    for c in range(d_ff // fc):
        gate = _dot(xn, wgu_ref[:, c * fc:(c + 1) * fc])
        up = _dot(xn, wgu_ref[:, d_ff + c * fc:d_ff + (c + 1) * fc])
        h_ref[:, c * fc:(c + 1) * fc] = (_silu(gate) * up).astype(BF16)
    o_ref[...] = x1 + _dot(h_ref[...], wd_ref[...])


def _ffn(x2, mix_a, mix_b, wo_a, wo_b, gain, w_gu, w_down, tm, fc):
    n, d = x2.shape
    d_ff = w_down.shape[0]
    row = lambda i: (i, 0)
    const = lambda i: (0, 0)
    rs = lambda w: pl.BlockSpec((tm, w), row)
    full = lambda a: pl.BlockSpec(a.shape, const, pipeline_mode=pl.Buffered(1))
    return pl.pallas_call(
        functools.partial(_ffn_kernel, fc=fc),
        grid=(n // tm,),
        in_specs=[rs(d), rs(mix_a.shape[1]), rs(mix_b.shape[1]), full(wo_a), full(wo_b), full(gain),
                  full(w_gu), full(w_down)],
        out_specs=rs(d),
        out_shape=jax.ShapeDtypeStruct((n, d), F32),
        scratch_shapes=[pltpu.VMEM((tm, d_ff), BF16)],
        compiler_params=pltpu.CompilerParams(
            dimension_semantics=("arbitrary",), vmem_limit_bytes=VMEM_LIMIT),
        name="outproj_ffn",
    )(x2, mix_a, mix_b, wo_a, wo_b, gain, w_gu, w_down)


def _row(v):
    return v.reshape(1, -1).astype(F32)


def _pad_lanes(v, width, offset=0):
    out = jnp.zeros((1, width), F32)
    return out.at[0, offset:offset + v.shape[-1]].set(v.astype(F32))


def _pad_heads(w, heads, src, dst):
    rows = w.shape[0]
    w = w.reshape(rows, heads, src)
    return jnp.pad(w, ((0, 0), (0, 0), (0, dst - src))).reshape(rows, heads * dst)


def _tile(tlen, want):
    t = min(want, tlen)
    assert tlen % t == 0, (tlen, t)
    return t


def kernel(x, positions, norm_mix, norm_ffn, ffn_w_gu, ffn_w_down, ev_w_in, ev_conv_a, ev_conv_qkv, ev_a_log, ev_dt_bias, ev_out_norm, ev_w_out, od_w_in, od_shift_mu, od_w0, od_w2, od_a0, od_a2, od_g2, od_k_k, od_k_a, od_r_k, od_lnx_w, od_lnx_b, od_vres_w1, od_vres_mu, od_vres_v0, od_vres_v2, od_qa_norm, od_kva_norm, od_w_uq, od_w_ukv, od_q_ln, od_k_ln, od_w_out):
    bsz, tlen, d = x.shape
    n = bsz * tlen
    depth = norm_mix.shape[0]
    tm_proj = _tile(tlen, 256)
    tm_ffn = _tile(tlen, 512)
    tt_gdn = _tile(tlen, 128)
    tt_rwkv = _tile(tlen, 256)
    tq_mla = _tile(tlen, 512)
    fc = 256

    x2 = x.reshape(n, d)
    tables = _rope_tables(positions, tm_proj)
    v_first = None

    for layer in range(depth):
        gain = _row(norm_mix[layer])
        if layer % 2 == 0:
            e = layer // 2
            n_main = 3 * A_WIDTH + 4 * GDN_WIDTH
            w_main = ev_w_in[e][:, :n_main].astype(BF16)
            w_small = jnp.pad(ev_w_in[e][:, n_main:], ((0, 0), (0, LANES - 2 * GDN_HEADS))).astype(BF16)
            alog_l = _pad_lanes(ev_a_log[e], LANES, GDN_HEADS)
            dtb_l = _pad_lanes(ev_dt_bias[e], LANES, GDN_HEADS)
            ya, q, k, v, gate, bg = _ev_in(x2, gain, w_main, w_small, ev_conv_a[e].astype(F32),
                                           ev_conv_qkv[e].astype(F32), alog_l, dtb_l, bsz, tlen, tm_proj)
            g_pairs = bg[:, GDN_HEADS:2 * GDN_HEADS].reshape(n // GDN_CHUNK, GDN_CHUNK, GDN_HEADS // 2, 2)
            g_pairs = g_pairs.transpose(0, 2, 3, 1).reshape(n // GDN_CHUNK, GDN_HEADS // 2, 2 * GDN_CHUNK)
            bgt = jnp.pad(g_pairs, ((0, 0), (0, SUBLANES - GDN_HEADS // 2), (0, 0)))
            o = _gdn(q, k, v, bg, bgt, gate, _row(ev_out_norm[e]), bsz, tlen, tt_gdn)
            mix_a, mix_b = ya, o
            w_out = ev_w_out[e].astype(BF16)
            wo_a, wo_b = w_out[:A_WIDTH], w_out[A_WIDTH:]
        else:
            o = layer // 2
            has_vres = o > 0
            w_in = od_w_in[o]
            extra = jnp.zeros((d, LANES), F32).at[:, :MLA_ROPE].set(w_in[:, RWKV_SHIFT_COLS + MLA_Q_LORA + MLA_KV_LORA:])
            mu = jnp.zeros((1, OD_COLS), F32).at[0, :RWKV_SHIFT_COLS].set(od_shift_mu[o])
            if has_vres:
                extra = extra.at[:, MLA_ROPE:MLA_ROPE + RWKV_V_LORA].set(od_vres_w1[o - 1])
                mu = mu.at[0, OD_X + MLA_ROPE:OD_X + MLA_ROPE + RWKV_V_LORA].set(od_vres_mu[o - 1])
            n_mla = MLA_Q_LORA + MLA_KV_LORA
            w_packed = jnp.concatenate(
                [w_in[:, :RWKV_SHIFT_COLS], extra, w_in[:, RWKV_SHIFT_COLS:RWKV_SHIFT_COLS + n_mla]], axis=1).astype(BF16)
            wa2 = jnp.zeros((LANES, 2 * RWKV_WIDTH), F32)
            wa2 = wa2.at[:RWKV_W_LORA, :RWKV_WIDTH].set(od_w2[o]).at[RWKV_W_LORA:, RWKV_WIDTH:].set(od_a2[o]).astype(BF16)
            wuq = _pad_heads(od_w_uq[o], MLA_HEADS, MLA_QK, MLA_PAD).astype(BF16)
            w_ukv = od_w_ukv[o].reshape(MLA_KV_LORA, MLA_HEADS, MLA_NOPE + MLA_V)
            wuk = _pad_heads(w_ukv[:, :, :MLA_NOPE].reshape(MLA_KV_LORA, -1), MLA_HEADS, MLA_NOPE, MLA_PAD).astype(BF16)
            wuv = w_ukv[:, :, MLA_NOPE:].reshape(MLA_KV_LORA, -1).astype(BF16)
            vres = None
            if has_vres:
                v2 = jnp.zeros((LANES, RWKV_WIDTH), F32).at[MLA_ROPE:MLA_ROPE + RWKV_V_LORA].set(od_vres_v2[o - 1])
                vres = (v_first, _row(od_vres_v0[o - 1]), v2.astype(BF16))
            (r, lw, k, v, kk, b, g, q_mla, k_mla, v_mla) = _od_in(
                x2, gain, w_packed, mu, wa2, _row(od_w0[o]), _row(od_a0[o]), od_g2[o].astype(BF16),
                _row(od_k_k[o]), _row(od_k_a[o]), _row(od_qa_norm[o]), _row(od_kva_norm[o]), wuq, wuk, wuv,
                _pad_lanes(od_q_ln[o], MLA_PAD), _pad_lanes(od_k_ln[o], MLA_PAD), tables, vres, bsz, tlen, tm_proj)
            if not has_vres:
                v_first = v
            y_rwkv = _rwkv(r, lw, k, v, kk, b, g, _row(od_r_k[o]), _row(od_lnx_w[o]), _row(od_lnx_b[o]),
                           bsz, tlen, tt_rwkv)
            o_mla = _mla(q_mla, k_mla, v_mla, bsz, tlen, tq_mla)
            mix_a, mix_b = y_rwkv, o_mla
            w_out = od_w_out[o].astype(BF16)
            wo_a, wo_b = w_out[:RWKV_WIDTH], w_out[RWKV_WIDTH:]
        x2 = _ffn(x2, mix_a, mix_b, wo_a, wo_b, _row(norm_ffn[layer]), ffn_w_gu[layer].astype(BF16),
                  ffn_w_down[layer].astype(BF16), tm_ffn, fc)
    return x2.reshape(bsz, tlen, d)
```

```python
import functools

import jax
import jax.numpy as jnp
from jax import lax
from jax.experimental import pallas as pl
from jax.experimental.pallas import tpu as pltpu

F32 = jnp.float32
BF16 = jnp.bfloat16
HIGHEST = lax.Precision.HIGHEST

LANES = 128
SUBLANES = 8
VMEM_LIMIT = 56 * 1024 * 1024

RMS_EPS = 1e-6
L2_EPS = 1e-6
RWKV_LN_EPS = 64e-5
ROPE_THETA = 10000.0

A_WIDTH = 256
GDN_HEADS = 6
GDN_DIM = 128
GDN_WIDTH = GDN_HEADS * GDN_DIM
GDN_CHUNK = 64
GDN_BLOCK = 16

RWKV_HEADS = 8
RWKV_DIM = 64
RWKV_WIDTH = RWKV_HEADS * RWKV_DIM
RWKV_PAIRS = RWKV_WIDTH // LANES
RWKV_CHUNK = 64
RWKV_W_LORA = 64
RWKV_A_LORA = 64
RWKV_V_LORA = 32
RWKV_G_LORA = 128
RWKV_SHIFT_COLS = 3 * RWKV_WIDTH + RWKV_W_LORA + RWKV_A_LORA + RWKV_G_LORA

MLA_HEADS = 8
MLA_NOPE = 64
MLA_ROPE = 32
MLA_V = 64
MLA_QK = MLA_NOPE + MLA_ROPE
MLA_Q_LORA = 512
MLA_KV_LORA = 256
MLA_PAD = 128

HALO = SUBLANES


def _dot(a, b, precision=None):
    return jnp.dot(a, b, preferred_element_type=F32, precision=precision)


def _dot_nt(a, b, precision=None):
    return lax.dot_general(a, b, (((1,), (1,)), ((), ())), preferred_element_type=F32, precision=precision)


def _dot_tn(a, b, precision=None):
    return lax.dot_general(a, b, (((0,), (0,)), ((), ())), preferred_element_type=F32, precision=precision)


def _rms(x, gain):
    return x * lax.rsqrt(jnp.mean(x * x, axis=-1, keepdims=True) + RMS_EPS) * gain


def _sigmoid(x):
    return 1.0 / (1.0 + jnp.exp(-x))


def _silu(x):
    return x * _sigmoid(x)


def _softplus(x):
    return jnp.maximum(x, 0.0) + jnp.log(1.0 + jnp.exp(-jnp.abs(x)))


def _seg64_sum(x):
    lo = lax.broadcasted_iota(jnp.int32, x.shape, 1) < RWKV_DIM
    s_lo = jnp.sum(jnp.where(lo, x, 0.0), axis=-1, keepdims=True)
    s_hi = jnp.sum(jnp.where(lo, 0.0, x), axis=-1, keepdims=True)
    return jnp.where(lo, s_lo, s_hi)


def _shifted_rows(z, scr_ref, carry_ref, first, shifts):
    tm, width = z.shape
    assert width >= 2 * LANES, width
    prev = carry_ref[...]
    scr_ref[0:HALO, :] = jnp.where(first, jnp.zeros_like(prev), prev)
    scr_ref[HALO:HALO + tm, :] = z
    carry_ref[...] = z[tm - HALO:, :]
    return [scr_ref[pl.ds(HALO - s, tm), :] for s in shifts]


def _rope_kernel(pos_ref, invf_ref, c_ref, sa_ref, sb_ref):
    ang = pos_ref[...] * invf_ref[...]
    lane = lax.broadcasted_iota(jnp.int32, ang.shape, 1)
    cos = jnp.cos(ang)
    sin = jnp.sin(ang)
    half = MLA_ROPE // 2
    c_ref[...] = jnp.where(lane < MLA_NOPE, 1.0, jnp.where(lane < MLA_QK, cos, 0.0))
    sa_ref[...] = jnp.where((lane >= MLA_NOPE) & (lane < MLA_NOPE + half), -sin, 0.0)
    sb_ref[...] = jnp.where((lane >= MLA_NOPE + half) & (lane < MLA_QK), sin, 0.0)


def _rope_tables(positions, tm):
    n = positions.size
    pos = jnp.broadcast_to(positions.reshape(n, 1).astype(F32), (n, LANES))
    inv_freq = ROPE_THETA ** (-jnp.arange(0, MLA_ROPE, 2, dtype=F32) / MLA_ROPE)
    half = MLA_ROPE // 2
    invf = jnp.zeros((1, LANES), F32)
    invf = invf.at[0, MLA_NOPE:MLA_NOPE + half].set(inv_freq).at[0, MLA_NOPE + half:MLA_QK].set(inv_freq)
    spec = pl.BlockSpec((tm, LANES), lambda i: (i, 0))
    out = jax.ShapeDtypeStruct((n, LANES), F32)
    return pl.pallas_call(
        _rope_kernel,
        grid=(n // tm,),
        in_specs=[spec, pl.BlockSpec((1, LANES), lambda i: (0, 0))],
        out_specs=[spec, spec, spec],
        out_shape=[out, out, out],
        name="rope_tables",
    )(pos, invf)


def _ev_in_kernel(x_ref, gain_ref, w_ref, ws_ref, ca_ref, cq_ref, alog_ref, dtb_ref,
                  ya_ref, q_ref, k_ref, v_ref, gate_ref, bg_ref,
                  scr_ref, carry_p_ref, carry_qkv_ref):
    first = pl.program_id(1) == 0
    xn = _rms(x_ref[...], gain_ref[...]).astype(BF16)

    za = _dot(xn, w_ref[:, 0:3 * A_WIDTH])
    a_b = za[:, 0:A_WIDTH]
    prod = za[:, A_WIDTH:2 * A_WIDTH] * za[:, 2 * A_WIDTH:3 * A_WIDTH]
    p1, p2 = _shifted_rows(prod, scr_ref.at[:, 0:A_WIDTH], carry_p_ref, first, (1, 2))
    ca = ca_ref[...]
    ya = a_b * (ca[0:1, :] * p2 + ca[1:2, :] * p1 + ca[2:3, :] * prod)
    ya_ref[...] = ya.astype(ya_ref.dtype)

    base = 3 * A_WIDTH
    for g, out_ref in enumerate((q_ref, k_ref, v_ref)):
        cols = slice(base + g * GDN_WIDTH, base + (g + 1) * GDN_WIDTH)
        z = _dot(xn, w_ref[:, cols])
        z1, z2, z3 = _shifted_rows(z, scr_ref, carry_qkv_ref.at[g], first, (1, 2, 3))
        cw = cq_ref[:, g * GDN_WIDTH:(g + 1) * GDN_WIDTH]
        y = _silu(cw[0:1, :] * z3 + cw[1:2, :] * z2 + cw[2:3, :] * z1 + cw[3:4, :] * z)
        for h in range(GDN_HEADS):
            hs = slice(h * GDN_DIM, (h + 1) * GDN_DIM)
            yh = y[:, hs]
            if g < 2:
                yh = yh * lax.rsqrt(jnp.sum(yh * yh, axis=-1, keepdims=True) + L2_EPS)
            if g == 0:
                yh = yh * (GDN_DIM ** -0.5)
            out_ref[:, hs] = yh

    gcols = slice(base + 3 * GDN_WIDTH, base + 4 * GDN_WIDTH)
    gate_ref[...] = _dot(xn, w_ref[:, gcols])

    zs = _dot(xn, ws_ref[...])
    lane = lax.broadcasted_iota(jnp.int32, zs.shape, 1)
    g_log = -jnp.exp(alog_ref[...]) * _softplus(zs + dtb_ref[...])
    bg_ref[...] = jnp.where(lane < GDN_HEADS, _sigmoid(zs), g_log)


def _ev_in(x2, gain, w_main, w_small, conv_a, conv_qkv, alog_l, dtb_l, bsz, tlen, tm):
    n, d = x2.shape
    nt = tlen // tm
    row = lambda b, i: (b * nt + i, 0)
    const = lambda b, i: (0, 0)
    rs = lambda w: pl.BlockSpec((tm, w), row)
    full = lambda a: pl.BlockSpec(a.shape, const)
    outs = [
        jax.ShapeDtypeStruct((n, A_WIDTH), BF16),
        jax.ShapeDtypeStruct((n, GDN_WIDTH), F32),
        jax.ShapeDtypeStruct((n, GDN_WIDTH), F32),
        jax.ShapeDtypeStruct((n, GDN_WIDTH), F32),
        jax.ShapeDtypeStruct((n, GDN_WIDTH), F32),
        jax.ShapeDtypeStruct((n, LANES), F32),
    ]
    return pl.pallas_call(
        _ev_in_kernel,
        grid=(bsz, nt),
        in_specs=[rs(d), full(gain), full(w_main), full(w_small), full(conv_a), full(conv_qkv),
                  full(alog_l), full(dtb_l)],
        out_specs=[rs(A_WIDTH), rs(GDN_WIDTH), rs(GDN_WIDTH), rs(GDN_WIDTH), rs(GDN_WIDTH), rs(LANES)],
        out_shape=outs,
        scratch_shapes=[
            pltpu.VMEM((HALO + tm, GDN_WIDTH), F32),
            pltpu.VMEM((HALO, A_WIDTH), F32),
            pltpu.VMEM((3, HALO, GDN_WIDTH), F32),
        ],
        compiler_params=pltpu.CompilerParams(
            dimension_semantics=("arbitrary", "arbitrary"), vmem_limit_bytes=VMEM_LIMIT),
        name="even_in_proj",
    )(x2, gain, w_main, w_small, conv_a, conv_qkv, alog_l, dtb_l)


def _unit_lower_inverse_many(l_mats, row, col, dot):
    eye = (row == col).astype(F32)
    in_block = (row // GDN_BLOCK) == (col // GDN_BLOCK)
    l_diag = [jnp.where(in_block, m, 0.0) for m in l_mats]
    l_off = [jnp.where(in_block, 0.0, m) for m in l_mats]
    x2 = [dot(m, m) for m in l_diag]
    x4 = [dot(m, m) for m in x2]
    x8 = [dot(m, m) for m in x4]
    d_inv = [eye - m for m in l_diag]
    d_inv = [m + dot(m, x) for m, x in zip(d_inv, x2)]
    d_inv = [m + dot(m, x) for m, x in zip(d_inv, x4)]
    d_inv = [m + dot(m, x) for m, x in zip(d_inv, x8)]
    n1 = [dot(m, x) for m, x in zip(d_inv, l_off)]
    n2 = [dot(m, m) for m in n1]
    outer = [eye - m for m in n1]
    outer = [m + dot(m, x) for m, x in zip(outer, n2)]
    return [dot(m, x) for m, x in zip(outer, d_inv)]


def _gdn_kernel(q_ref, k_ref, v_ref, bg_ref, gate_ref, onorm_ref, o_ref, s_ref, *, nchunk):
    @pl.when(pl.program_id(1) == 0)
    def _():
        s_ref[...] = jnp.zeros_like(s_ref)

    c = GDN_CHUNK
    dk = GDN_DIM
    pairs = range(GDN_HEADS // 2)
    row_c = lax.broadcasted_iota(jnp.int32, (c, c), 0)
    col_c = lax.broadcasted_iota(jnp.int32, (c, c), 1)
    tri = (row_c >= col_c).astype(F32)
    row1 = lax.broadcasted_iota(jnp.int32, (2 * c, 2 * c), 0)
    col1 = lax.broadcasted_iota(jnp.int32, (2 * c, 2 * c), 1)
    same_head = (row1 // c) == (col1 // c)
    lag = (row1 % c) - (col1 % c)
    causal = same_head & (lag >= 0)
    strict = same_head & (lag > 0)
    tri_dup = (lax.broadcasted_iota(jnp.int32, (c, 2 * c), 0) <= lax.broadcasted_iota(jnp.int32, (c, 2 * c), 1) % c).astype(F32)
    first_half = lax.broadcasted_iota(jnp.int32, (1, 2 * c), 1) < c
    head0_rows = (lax.broadcasted_iota(jnp.int32, (2 * c, 1), 0) < c).astype(F32)
    onorm = onorm_ref[...]

    def bdot(a, b):
        return _dot(a.astype(BF16), b.astype(BF16))

    def rows2(a, b):
        return jnp.concatenate([a, b], axis=0)

    items = [(ci, p) for ci in range(nchunk) for p in pairs]
    q_s, k_s, v_s, beta_s, gc_s, gl_s, l_mats, qk = [], [], [], [], [], [], [], []
    for ci in range(nchunk):
        rows = slice(ci * c, (ci + 1) * c)
        bg = bg_ref[rows, :]
        g_cols = _dot(tri, bg, precision=HIGHEST)
        g_rows = _dot_tn(bg, tri_dup, precision=HIGHEST)
        for p in pairs:
            h0, h1 = 2 * p, 2 * p + 1
            s0, s1 = slice(h0 * dk, (h0 + 1) * dk), slice(h1 * dk, (h1 + 1) * dk)
            q_s.append(rows2(q_ref[rows, s0], q_ref[rows, s1]))
            k_s.append(rows2(k_ref[rows, s0], k_ref[rows, s1]))
            v_s.append(rows2(v_ref[rows, s0], v_ref[rows, s1]))
            beta_s.append(rows2(bg[:, h0:h0 + 1], bg[:, h1:h1 + 1]))
            gc0 = g_cols[:, GDN_HEADS + h0:GDN_HEADS + h0 + 1]
            gc1 = g_cols[:, GDN_HEADS + h1:GDN_HEADS + h1 + 1]
            gc_s.append(rows2(gc0, gc1))
            gl_s.append(rows2(jnp.broadcast_to(gc0[c - 1:c, :], (c, 1)), jnp.broadcast_to(gc1[c - 1:c, :], (c, 1))))
            gr = jnp.where(first_half, g_rows[GDN_HEADS + h0:GDN_HEADS + h0 + 1, :], g_rows[GDN_HEADS + h1:GDN_HEADS + h1 + 1, :])
            diff = gc_s[-1] - gr
            decay = jnp.where(causal, jnp.exp(jnp.where(causal, diff, 0.0)), 0.0)
            kb = k_s[-1] * beta_s[-1]
            score = _dot_nt(rows2(kb, q_s[-1]).astype(BF16), k_s[-1].astype(BF16))
            l_mats.append(jnp.where(strict, score[0:2 * c] * decay, 0.0))
            qk.append(score[2 * c:4 * c] * decay)
    t_inv = _unit_lower_inverse_many(l_mats, row1, col1, bdot)
    eg = [jnp.exp(g) for g in gc_s]
    wu = [bdot(t_inv[i], jnp.concatenate([k_s[i] * (beta_s[i] * eg[i]), v_s[i] * beta_s[i]], axis=1)) for i in range(len(items))]
    qy = [bdot(qk[i], wu[i]) for i in range(len(items))]
    gd = []
    for i in range(len(items)):
        k_dec = k_s[i] * jnp.exp(gl_s[i] - gc_s[i])
        wu0 = wu[i] * head0_rows
        wu1 = wu[i] - wu0
        expanded = jnp.concatenate([wu0[:, 0:dk], wu1[:, 0:dk], wu0[:, dk:], wu1[:, dk:]], axis=1)
        gd.append(_dot_tn(k_dec.astype(BF16), expanded.astype(BF16)))

    for i, (ci, p) in enumerate(items):
        rows = slice(ci * c, (ci + 1) * c)
        q_hat = q_s[i] * eg[i] - qy[i][:, 0:dk]
        y0 = qy[i][:, dk:]
        for hh in range(2):
            h = 2 * p + hh
            hs = slice(h * dk, (h + 1) * dk)
            hr = slice(hh * c, (hh + 1) * c)
            state = s_ref[h]
            o = bdot(q_hat[hr], state) + y0[hr]
            g_last = gl_s[i][hh * c:hh * c + 1, :]
            s_ref[h] = state * jnp.exp(g_last) - bdot(gd[i][:, hh * dk:(hh + 1) * dk], state) + gd[i][:, (2 + hh) * dk:(3 + hh) * dk]
            o = _rms(o, onorm) * _silu(gate_ref[rows, hs])
            o_ref[rows, hs] = o.astype(o_ref.dtype)


def _gdn(q, k, v, bg, gate, onorm, bsz, tlen, tt):
    n = q.shape[0]
    nt = tlen // tt
    nchunk = tt // GDN_CHUNK
    row = lambda b, i: (b * nt + i, 0)
    rs = lambda w: pl.BlockSpec((tt, w), row)
    return pl.pallas_call(
        functools.partial(_gdn_kernel, nchunk=nchunk),
        grid=(bsz, nt),
        in_specs=[rs(GDN_WIDTH), rs(GDN_WIDTH), rs(GDN_WIDTH), rs(LANES),
                  rs(GDN_WIDTH), pl.BlockSpec(onorm.shape, lambda b, i: (0, 0))],
        out_specs=rs(GDN_WIDTH),
        out_shape=jax.ShapeDtypeStruct((n, GDN_WIDTH), BF16),
        scratch_shapes=[pltpu.VMEM((GDN_HEADS, GDN_DIM, GDN_DIM), F32)],
        compiler_params=pltpu.CompilerParams(
            dimension_semantics=("arbitrary", "arbitrary"), vmem_limit_bytes=VMEM_LIMIT),
        name="gated_delta_rule",
    )(q, k, v, bg, gate, onorm)


OD_R = 0
OD_K = RWKV_WIDTH
OD_V = 2 * RWKV_WIDTH
OD_WA = 3 * RWKV_WIDTH
OD_G = OD_WA + LANES
OD_X = RWKV_SHIFT_COLS
OD_SMALL = 3 * LANES
OD_CQ = OD_X + LANES
OD_CKV = OD_CQ + MLA_Q_LORA
OD_COLS = OD_CKV + MLA_KV_LORA


def _od_in_kernel(*refs, has_vres):
    (x_ref, gain_ref, w_ref, mu_ref, wa2_ref, w0_ref, a0_ref, g2_ref, kk_ref, ka_ref,
     qan_ref, kvan_ref, wuq_ref, wuk_ref, wuv_ref, qln_ref, kln_ref,
     c_ref, sa_ref, sb_ref) = refs[:20]
    pos = 20
    if has_vres:
        vfirst_ref, v0_ref, v2_ref = refs[pos:pos + 3]
        pos += 3
    (r_out, lw_out, k_out, v_out, kkn_out, b_out, g_out, q_out, kmla_out, vmla_out) = refs[pos:pos + 10]
    scr_ref, carry_ref = refs[pos + 10:]

    first = pl.program_id(1) == 0
    xn = _rms(x_ref[...], gain_ref[...]).astype(BF16)

    def lerped(start, width, slot):
        z = _dot(xn, w_ref[:, start:start + width])
        (zp,) = _shifted_rows(z, scr_ref.at[:, 0:width], carry_ref.at[:, slot:slot + width], first, (1,))
        return z, z + mu_ref[:, start:start + width] * (zp - z)

    _, r = lerped(OD_R, RWKV_WIDTH, OD_R)
    r_out[...] = r

    small, small_l = lerped(OD_WA, OD_SMALL, OD_WA)
    wa = small_l[:, 0:LANES]
    gd = small_l[:, LANES:2 * LANES]
    zx = small[:, 2 * LANES:3 * LANES]
    zx_l = small_l[:, 2 * LANES:3 * LANES]
    lane = lax.broadcasted_iota(jnp.int32, wa.shape, 1)
    lora_in = jnp.where(lane < RWKV_W_LORA, jnp.tanh(wa), wa).astype(BF16)
    lora = _dot(lora_in, wa2_ref[...])
    w_log = -_softplus(-(w0_ref[...] + lora[:, 0:RWKV_WIDTH])) - 0.5
    lw_out[...] = -jnp.exp(w_log)
    a = _sigmoid(a0_ref[...] + lora[:, RWKV_WIDTH:2 * RWKV_WIDTH])

    _, k = lerped(OD_K, RWKV_WIDTH, OD_K)
    kx = k * kk_ref[...]
    for p in range(RWKV_PAIRS):
        ps = slice(p * LANES, (p + 1) * LANES)
        kxp = kx[:, ps]
        kkp = kxp * lax.rsqrt(_seg64_sum(kxp * kxp) + L2_EPS)
        kkn_out[:, ps] = kkp
        b_out[:, ps] = kkp * a[:, ps]
    k_out[...] = k * (1.0 + (a - 1.0) * ka_ref[...])

    _, v = lerped(OD_V, RWKV_WIDTH, OD_V)
    if has_vres:
        mix = _sigmoid(v0_ref[...] + _dot(zx_l.astype(BF16), v2_ref[...]))
        v = v + (vfirst_ref[...] - v) * mix
    v_out[...] = v

    g_out[...] = _dot(_sigmoid(gd).astype(BF16), g2_ref[...])

    cos_t = c_ref[...]
    sin_a = sa_ref[...]
    sin_b = sb_ref[...]

    def rope(t):
        return t * cos_t + pltpu.roll(t, LANES - MLA_ROPE // 2, 1) * sin_a + pltpu.roll(t, MLA_ROPE // 2, 1) * sin_b

    cq = _dot(xn, w_ref[:, OD_CQ:OD_CQ + MLA_Q_LORA])
    q_all = _dot(_rms(cq, qan_ref[...]).astype(BF16), wuq_ref[...])
    ckv = _dot(xn, w_ref[:, OD_CKV:OD_CKV + MLA_KV_LORA])
    ckv_n = _rms(ckv, kvan_ref[...]).astype(BF16)
    k_all = _dot(ckv_n, wuk_ref[...])
    vmla_out[...] = _dot(ckv_n, wuv_ref[...]).astype(vmla_out.dtype)
    lane = lax.broadcasted_iota(jnp.int32, zx.shape, 1)
    k_rope = jnp.where((lane >= MLA_NOPE) & (lane < MLA_QK), pltpu.roll(zx, MLA_NOPE, 1), 0.0)
    q_ln = qln_ref[...]
    k_ln = kln_ref[...]
    scale = MLA_QK ** -0.5
    for h in range(MLA_HEADS):
        hs = slice(h * MLA_PAD, (h + 1) * MLA_PAD)
        qh = q_all[:, hs]
        qh = qh * lax.rsqrt(jnp.sum(qh * qh, axis=-1, keepdims=True) * (1.0 / MLA_QK) + RMS_EPS) * q_ln
        q_out[:, hs] = (rope(qh) * scale).astype(q_out.dtype)
        kh = k_all[:, hs] + k_rope
        kh = kh * lax.rsqrt(jnp.sum(kh * kh, axis=-1, keepdims=True) * (1.0 / MLA_QK) + RMS_EPS) * k_ln
        kmla_out[:, hs] = rope(kh).astype(kmla_out.dtype)


def _od_in(x2, gain, w_in, mu, wa2, w0, a0, g2, k_k, k_a, qan, kvan, wuq, wuk, wuv, qln, kln,
           tables, vres, bsz, tlen, tm):
    n, d = x2.shape
    nt = tlen // tm
    row = lambda b, i: (b * nt + i, 0)
    const = lambda b, i: (0, 0)
    rs = lambda w: pl.BlockSpec((tm, w), row)
    full = lambda a: pl.BlockSpec(a.shape, const)
    params = [gain, w_in, mu, wa2, w0, a0, g2, k_k, k_a, qan, kvan, wuq, wuk, wuv, qln, kln]
    args = [x2] + params + list(tables)
    in_specs = [rs(d)] + [full(a) for a in params] + [rs(LANES)] * 3
    if vres is not None:
        v_first, v0, v2 = vres
        args += [v_first, v0, v2]
        in_specs += [rs(RWKV_WIDTH), full(v0), full(v2)]
    f32w = lambda w: jax.ShapeDtypeStruct((n, w), F32)
    bfw = lambda w: jax.ShapeDtypeStruct((n, w), BF16)
    outs = [f32w(RWKV_WIDTH)] * 7 + [bfw(MLA_HEADS * MLA_PAD), bfw(MLA_HEADS * MLA_PAD), bfw(MLA_HEADS * MLA_V)]
    out_specs = [rs(RWKV_WIDTH)] * 7 + [rs(MLA_HEADS * MLA_PAD), rs(MLA_HEADS * MLA_PAD), rs(MLA_HEADS * MLA_V)]
    carry_w = RWKV_SHIFT_COLS + LANES
    return pl.pallas_call(
        functools.partial(_od_in_kernel, has_vres=vres is not None),
        grid=(bsz, nt),
        in_specs=in_specs,
        out_specs=out_specs,
        out_shape=outs,
        scratch_shapes=[pltpu.VMEM((HALO + tm, RWKV_WIDTH), F32), pltpu.VMEM((HALO, carry_w), F32)],
        compiler_params=pltpu.CompilerParams(
            dimension_semantics=("arbitrary", "arbitrary"), vmem_limit_bytes=VMEM_LIMIT),
        name="odd_in_proj",
    )(*args)


def _rwkv_kernel(r_ref, lw_ref, k_ref, v_ref, kk_ref, b_ref, g_ref, rk_ref, lnw_ref, lnb_ref,
                 o_ref, h_ref, y_ref, *, nchunk):
    @pl.when(pl.program_id(1) == 0)
    def _():
        h_ref[...] = jnp.zeros_like(h_ref)

    c = RWKV_CHUNK
    d = RWKV_DIM
    lane = lax.broadcasted_iota(jnp.int32, (1, LANES), 1)
    m_lo = (lane < d).astype(F32)
    m_hi = 1.0 - m_lo
    row_c = lax.broadcasted_iota(jnp.int32, (c, c), 0)
    col_c = lax.broadcasted_iota(jnp.int32, (c, c), 1)
    tri = (row_c >= col_c).astype(F32)
    row1 = lax.broadcasted_iota(jnp.int32, (2 * c, 2 * c), 0)
    col1 = lax.broadcasted_iota(jnp.int32, (2 * c, 2 * c), 1)
    eye1 = row1 == col1
    row2 = lax.broadcasted_iota(jnp.int32, (4 * c, 4 * c), 0)
    col2 = lax.broadcasted_iota(jnp.int32, (4 * c, 4 * c), 1)
    same_head = ((row2 // c) % 2) == ((col2 // c) % 2)
    min_lag = jnp.where(row2 < 2 * c, 1, 0)
    keep = same_head & ((row2 % c) - (col2 % c) >= min_lag)

    def bdot(a, b):
        return _dot(a.astype(BF16), b.astype(BF16))

    def stack(x):
        return jnp.concatenate([x * m_lo, x * m_hi], axis=0)

    pairs = range(RWKV_PAIRS)
    ps = [slice(p * LANES, (p + 1) * LANES) for p in pairs]
    zero = jnp.zeros((2 * c, LANES), F32)
    items = [(ci, p) for ci in range(nchunk) for p in pairs]
    n_items = range(len(items))
    qt_p, w_last_p, am_s, v_s, lhs1, rhs1, lhs_t = [], [], [], [], [], [], []
    for ci in range(nchunk):
        rows = slice(ci * c, (ci + 1) * c)
        lw = lw_ref[rows, :]
        logw = _dot(tri, lw, precision=HIGHEST)
        logw_last = logw[c - 1:c, :]
        w_inv = jnp.exp(-logw)
        w_last = jnp.exp(logw_last)
        to_end = jnp.exp(logw_last - logw)
        am = kk_ref[rows, :] * jnp.exp(logw - lw)
        qt = r_ref[rows, :] * jnp.exp(logw)
        bp = b_ref[rows, :] * w_inv
        kp = k_ref[rows, :] * w_inv
        bh = b_ref[rows, :] * to_end
        kh = k_ref[rows, :] * to_end
        v = v_ref[rows, :]
        for s in ps:
            qt_p.append(qt[:, s])
            w_last_p.append(w_last[:, s])
            am_s.append(stack(am[:, s]))
            v_s.append(stack(v[:, s]))
            lhs1.append(jnp.concatenate([am_s[-1], stack(qt[:, s])], axis=0).astype(BF16))
            rhs1.append(jnp.concatenate([bp[:, s], bp[:, s], kp[:, s], kp[:, s]], axis=0).astype(BF16))
            lhs_t.append(jnp.concatenate([stack(bh[:, s]), stack(kh[:, s])], axis=0).astype(BF16))
    score = [jnp.where(keep, _dot_nt(lhs1[i], rhs1[i]), 0.0) for i in n_items]
    t_inv = _unit_lower_inverse_many([sc[0:2 * c, 0:2 * c] for sc in score], row1, col1, bdot)
    r1 = [bdot(score[i][0:2 * c, 2 * c:4 * c], v_s[i]) for i in n_items]
    x = [bdot(t_inv[i], jnp.concatenate([am_s[i], r1[i]], axis=1)) for i in n_items]
    rhs2 = [jnp.concatenate([jnp.concatenate([x[i][:, 0:LANES], -x[i][:, LANES:]], axis=1),
                             jnp.concatenate([zero, v_s[i]], axis=1)], axis=0).astype(BF16) for i in n_items]
    qy = [_dot(score[i][2 * c:4 * c, :].astype(BF16), rhs2[i]) for i in n_items]
    gd = [_dot_tn(lhs_t[i], rhs2[i]) for i in n_items]

    for i, (ci, p) in enumerate(items):
        rows = slice(ci * c, (ci + 1) * c)
        q_hat = qt_p[i] - (qy[i][0:c, 0:LANES] + qy[i][c:2 * c, 0:LANES])
        y0 = qy[i][0:c, LANES:] + qy[i][c:2 * c, LANES:]
        h = h_ref[p]
        y_ref[rows, ps[p]] = bdot(q_hat, h) + y0
        wl_col = jnp.sum(jnp.where(eye1, w_last_p[i], 0.0), axis=1, keepdims=True)
        h_ref[p] = wl_col * h - bdot(gd[i][:, 0:LANES], h) + gd[i][:, LANES:]

    for p in pairs:
        s = slice(p * LANES, (p + 1) * LANES)
        y = y_ref[:, s]
        mu = _seg64_sum(y) * (1.0 / d)
        yc = y - mu
        var = _seg64_sum(yc * yc) * (1.0 / d)
        y = yc * lax.rsqrt(var + RWKV_LN_EPS) * lnw_ref[:, s] + lnb_ref[:, s]
        bonus = _seg64_sum(r_ref[:, s] * k_ref[:, s] * rk_ref[:, s])
        y = (y + bonus * v_ref[:, s]) * g_ref[:, s]
        o_ref[:, s] = y.astype(o_ref.dtype)


def _rwkv(r, lw, k, v, kk, b, g, rk, lnw, lnb, bsz, tlen, tt):
    n, width = r.shape
    nt = tlen // tt
    blk = pl.BlockSpec((tt, width), lambda bb, i: (bb * nt + i, 0))
    par = pl.BlockSpec((1, width), lambda bb, i: (0, 0))
    return pl.pallas_call(
        functools.partial(_rwkv_kernel, nchunk=tt // RWKV_CHUNK),
        grid=(bsz, nt),
        in_specs=[blk] * 7 + [par] * 3,
        out_specs=blk,
        out_shape=jax.ShapeDtypeStruct((n, width), BF16),
        scratch_shapes=[pltpu.VMEM((RWKV_PAIRS, LANES, LANES), F32), pltpu.VMEM((tt, width), F32)],
        compiler_params=pltpu.CompilerParams(
            dimension_semantics=("arbitrary", "arbitrary"), vmem_limit_bytes=VMEM_LIMIT),
        name="rwkv7_chunked",
    )(r, lw, k, v, kk, b, g, rk, lnw, lnb)


def _mla_kernel(q_ref, k_ref, v_ref, o_ref, m_ref, l_ref, acc_ref, *, tq, tk):
    qi = pl.program_id(2)
    ntile = tk // LANES
    row = lax.broadcasted_iota(jnp.int32, (tq, LANES), 0)
    lane = lax.broadcasted_iota(jnp.int32, (tq, LANES), 1)

    def block(h, q, j, diag_offset):
        hs = slice(h * MLA_PAD, (h + 1) * MLA_PAD)
        start = pl.multiple_of(j * tk, tk)
        s = _dot_nt(q, k_ref[pl.ds(start, tk), hs])
        tiles = [s[:, i * LANES:(i + 1) * LANES] for i in range(ntile)]
        if diag_offset is not None:
            tiles = [jnp.where(lane + (diag_offset + i * LANES) <= row, t, -jnp.inf) for i, t in enumerate(tiles)]
        m_prev = m_ref[h]
        m_tile = tiles[0]
        for t in tiles[1:]:
            m_tile = jnp.maximum(m_tile, t)
        m_new = jnp.maximum(m_prev, jnp.max(m_tile, axis=-1, keepdims=True))
        alpha = jnp.exp(m_prev - m_new)
        p_tiles = [jnp.exp(t - m_new) for t in tiles]
        p_sum = p_tiles[0]
        for t in p_tiles[1:]:
            p_sum = p_sum + t
        l_ref[h] = alpha * l_ref[h] + jnp.sum(p_sum, axis=-1, keepdims=True)
        p = jnp.concatenate([t.astype(BF16) for t in p_tiles], axis=1)
        acc_ref[h] = alpha * acc_ref[h] + _dot(p, v_ref[pl.ds(start, tk), :])
        m_ref[h] = m_new

    ratio = tq // tk
    qs = [q_ref[:, h * MLA_PAD:(h + 1) * MLA_PAD] for h in range(2)]
    for h in range(2):
        m_ref[h] = jnp.full((tq, LANES), -jnp.inf, F32)
        l_ref[h] = jnp.zeros((tq, LANES), F32)
        acc_ref[h] = jnp.zeros((tq, LANES), F32)

    def full_block(j, carry):
        for h in range(2):
            block(h, qs[h], j, None)
        return carry

    lax.fori_loop(0, qi * ratio, full_block, 0)
    for d in range(ratio):
        for h in range(2):
            block(h, qs[h], qi * ratio + d, d * tk)
    out = jnp.where(lane < MLA_V, acc_ref[0] / l_ref[0], acc_ref[1] / l_ref[1])
    o_ref[...] = out.astype(o_ref.dtype)


def _mla(q, k, v, bsz, tlen, tq, tk):
    n = q.shape[0]
    nq = tlen // tq
    hp = MLA_HEADS // 2
    stat = pltpu.VMEM((2, tq, LANES), F32)
    return pl.pallas_call(
        functools.partial(_mla_kernel, tq=tq, tk=tk),
        grid=(bsz, hp, nq),
        in_specs=[pl.BlockSpec((tq, 2 * MLA_PAD), lambda b, h, i: (b * nq + i, h)),
                  pl.BlockSpec((tlen, 2 * MLA_PAD), lambda b, h, i: (b, h)),
                  pl.BlockSpec((tlen, 2 * MLA_V), lambda b, h, i: (b, h))],
        out_specs=pl.BlockSpec((tq, 2 * MLA_V), lambda b, h, i: (b * nq + i, h)),
        out_shape=jax.ShapeDtypeStruct((n, MLA_HEADS * MLA_V), BF16),
        scratch_shapes=[stat, stat, stat],
        compiler_params=pltpu.CompilerParams(
            dimension_semantics=("arbitrary", "arbitrary", "arbitrary"), vmem_limit_bytes=VMEM_LIMIT),
        name="mla_attention",
    )(q, k, v)


def _ffn_kernel(x_ref, ma_ref, mb_ref, woa_ref, wob_ref, gain_ref, wgu_ref, wd_ref, o_ref, h_ref, *, fc):
    d_ff = wd_ref.shape[0]
    x1 = x_ref[...] + _dot(ma_ref[...], woa_ref[...]) + _dot(mb_ref[...], wob_ref[...])
    xn = _rms(x1, gain_ref[...]).astype(BF16)
    for c in range(d_ff // fc):
        gate = _dot(xn, wgu_ref[:, c * fc:(c + 1) * fc])
        up = _dot(xn, wgu_ref[:, d_ff + c * fc:d_ff + (c + 1) * fc])
        h_ref[:, c * fc:(c + 1) * fc] = (_silu(gate) * up).astype(BF16)
    o_ref[...] = x1 + _dot(h_ref[...], wd_ref[...])


def _ffn(x2, mix_a, mix_b, wo_a, wo_b, gain, w_gu, w_down, tm, fc):
    n, d = x2.shape
    d_ff = w_down.shape[0]
    row = lambda i: (i, 0)
    const = lambda i: (0, 0)
    rs = lambda w: pl.BlockSpec((tm, w), row)
    full = lambda a: pl.BlockSpec(a.shape, const, pipeline_mode=pl.Buffered(1))
    return pl.pallas_call(
        functools.partial(_ffn_kernel, fc=fc),
        grid=(n // tm,),
        in_specs=[rs(d), rs(mix_a.shape[1]), rs(mix_b.shape[1]), full(wo_a), full(wo_b), full(gain),
                  full(w_gu), full(w_down)],
        out_specs=rs(d),
        out_shape=jax.ShapeDtypeStruct((n, d), F32),
        scratch_shapes=[pltpu.VMEM((tm, d_ff), BF16)],
        compiler_params=pltpu.CompilerParams(
            dimension_semantics=("arbitrary",), vmem_limit_bytes=VMEM_LIMIT),
        name="outproj_ffn",
    )(x2, mix_a, mix_b, wo_a, wo_b, gain, w_gu, w_down)


def _row(v):
    return v.reshape(1, -1).astype(F32)


def _pad_lanes(v, width, offset=0):
    out = jnp.zeros((1, width), F32)
    return out.at[0, offset:offset + v.shape[-1]].set(v.astype(F32))


def _pad_heads(w, heads, src, dst):
    rows = w.shape[0]
    w = w.reshape(rows, heads, src)
    return jnp.pad(w, ((0, 0), (0, 0), (0, dst - src))).reshape(rows, heads * dst)


def _tile(tlen, want):
    t = min(want, tlen)
    assert tlen % t == 0, (tlen, t)
    return t


def kernel(x, positions, norm_mix, norm_ffn, ffn_w_gu, ffn_w_down, ev_w_in, ev_conv_a, ev_conv_qkv, ev_a_log, ev_dt_bias, ev_out_norm, ev_w_out, od_w_in, od_shift_mu, od_w0, od_w2, od_a0, od_a2, od_g2, od_k_k, od_k_a, od_r_k, od_lnx_w, od_lnx_b, od_vres_w1, od_vres_mu, od_vres_v0, od_vres_v2, od_qa_norm, od_kva_norm, od_w_uq, od_w_ukv, od_q_ln, od_k_ln, od_w_out):
    bsz, tlen, d = x.shape
    n = bsz * tlen
    depth = norm_mix.shape[0]
    tm_proj = _tile(tlen, 256)
    tm_ffn = _tile(tlen, 512)
    tt_gdn = _tile(tlen, 256)
    tt_rwkv = _tile(tlen, 256)
    tq_mla = _tile(tlen, 512)
    tk_mla = _tile(tq_mla, 512)
    fc = 256

    x2 = x.reshape(n, d)
    tables = _rope_tables(positions, tm_proj)
    v_first = None

    for layer in range(depth):
        gain = _row(norm_mix[layer])
        if layer % 2 == 0:
            e = layer // 2
            n_main = 3 * A_WIDTH + 4 * GDN_WIDTH
            w_main = ev_w_in[e][:, :n_main].astype(BF16)
            w_small = jnp.pad(ev_w_in[e][:, n_main:], ((0, 0), (0, LANES - 2 * GDN_HEADS))).astype(BF16)
            alog_l = _pad_lanes(ev_a_log[e], LANES, GDN_HEADS)
            dtb_l = _pad_lanes(ev_dt_bias[e], LANES, GDN_HEADS)
            ya, q, k, v, gate, bg = _ev_in(x2, gain, w_main, w_small, ev_conv_a[e].astype(F32),
                                           ev_conv_qkv[e].astype(F32), alog_l, dtb_l, bsz, tlen, tm_proj)
            o = _gdn(q, k, v, bg, gate, _row(ev_out_norm[e]), bsz, tlen, tt_gdn)
            mix_a, mix_b = ya, o
            w_out = ev_w_out[e].astype(BF16)
            wo_a, wo_b = w_out[:A_WIDTH], w_out[A_WIDTH:]
        else:
            o = layer // 2
            has_vres = o > 0
            w_in = od_w_in[o]
            extra = jnp.zeros((d, LANES), F32).at[:, :MLA_ROPE].set(w_in[:, RWKV_SHIFT_COLS + MLA_Q_LORA + MLA_KV_LORA:])
            mu = jnp.zeros((1, OD_COLS), F32).at[0, :RWKV_SHIFT_COLS].set(od_shift_mu[o])
            if has_vres:
                extra = extra.at[:, MLA_ROPE:MLA_ROPE + RWKV_V_LORA].set(od_vres_w1[o - 1])
                mu = mu.at[0, OD_X + MLA_ROPE:OD_X + MLA_ROPE + RWKV_V_LORA].set(od_vres_mu[o - 1])
            n_mla = MLA_Q_LORA + MLA_KV_LORA
            w_packed = jnp.concatenate(
                [w_in[:, :RWKV_SHIFT_COLS], extra, w_in[:, RWKV_SHIFT_COLS:RWKV_SHIFT_COLS + n_mla]], axis=1).astype(BF16)
            wa2 = jnp.zeros((LANES, 2 * RWKV_WIDTH), F32)
            wa2 = wa2.at[:RWKV_W_LORA, :RWKV_WIDTH].set(od_w2[o]).at[RWKV_W_LORA:, RWKV_WIDTH:].set(od_a2[o]).astype(BF16)
            wuq = _pad_heads(od_w_uq[o], MLA_HEADS, MLA_QK, MLA_PAD).astype(BF16)
            w_ukv = od_w_ukv[o].reshape(MLA_KV_LORA, MLA_HEADS, MLA_NOPE + MLA_V)
            wuk = _pad_heads(w_ukv[:, :, :MLA_NOPE].reshape(MLA_KV_LORA, -1), MLA_HEADS, MLA_NOPE, MLA_PAD).astype(BF16)
            wuv = w_ukv[:, :, MLA_NOPE:].reshape(MLA_KV_LORA, -1).astype(BF16)
            vres = None
            if has_vres:
                v2 = jnp.zeros((LANES, RWKV_WIDTH), F32).at[MLA_ROPE:MLA_ROPE + RWKV_V_LORA].set(od_vres_v2[o - 1])
                vres = (v_first, _row(od_vres_v0[o - 1]), v2.astype(BF16))
            (r, lw, k, v, kk, b, g, q_mla, k_mla, v_mla) = _od_in(
                x2, gain, w_packed, mu, wa2, _row(od_w0[o]), _row(od_a0[o]), od_g2[o].astype(BF16),
                _row(od_k_k[o]), _row(od_k_a[o]), _row(od_qa_norm[o]), _row(od_kva_norm[o]), wuq, wuk, wuv,
                _pad_lanes(od_q_ln[o], MLA_PAD), _pad_lanes(od_k_ln[o], MLA_PAD), tables, vres, bsz, tlen, tm_proj)
            if not has_vres:
                v_first = v
            y_rwkv = _rwkv(r, lw, k, v, kk, b, g, _row(od_r_k[o]), _row(od_lnx_w[o]), _row(od_lnx_b[o]),
                           bsz, tlen, tt_rwkv)
            o_mla = _mla(q_mla, k_mla, v_mla, bsz, tlen, tq_mla, tk_mla)
            mix_a, mix_b = y_rwkv, o_mla
            w_out = od_w_out[o].astype(BF16)
            wo_a, wo_b = w_out[:RWKV_WIDTH], w_out[RWKV_WIDTH:]
        x2 = _ffn(x2, mix_a, mix_b, wo_a, wo_b, _row(norm_ffn[layer]), ffn_w_gu[layer].astype(BF16),
                  ffn_w_down[layer].astype(BF16), tm_ffn, fc)
    return x2.reshape(bsz, tlen, d)
```

```python
import functools

import jax
import jax.numpy as jnp
from jax import lax
from jax.experimental import pallas as pl
from jax.experimental.pallas import tpu as pltpu

F32 = jnp.float32
BF16 = jnp.bfloat16
HIGHEST = lax.Precision.HIGHEST

LANES = 128
SUBLANES = 8
VMEM_LIMIT = 56 * 1024 * 1024

RMS_EPS = 1e-6
L2_EPS = 1e-6
RWKV_LN_EPS = 64e-5
ROPE_THETA = 10000.0

A_WIDTH = 256
GDN_HEADS = 6
GDN_DIM = 128
GDN_WIDTH = GDN_HEADS * GDN_DIM
GDN_CHUNK = 64
GDN_BLOCK = 16

RWKV_HEADS = 8
RWKV_DIM = 64
RWKV_WIDTH = RWKV_HEADS * RWKV_DIM
RWKV_PAIRS = RWKV_WIDTH // LANES
RWKV_CHUNK = 64
RWKV_W_LORA = 64
RWKV_A_LORA = 64
RWKV_V_LORA = 32
RWKV_G_LORA = 128
RWKV_SHIFT_COLS = 3 * RWKV_WIDTH + RWKV_W_LORA + RWKV_A_LORA + RWKV_G_LORA

MLA_HEADS = 8
MLA_NOPE = 64
MLA_ROPE = 32
MLA_V = 64
MLA_QK = MLA_NOPE + MLA_ROPE
MLA_Q_LORA = 512
MLA_KV_LORA = 256
MLA_PAD = 128

HALO = SUBLANES


def _dot(a, b, precision=None):
    return jnp.dot(a, b, preferred_element_type=F32, precision=precision)


def _dot_nt(a, b, precision=None):
    return lax.dot_general(a, b, (((1,), (1,)), ((), ())), preferred_element_type=F32, precision=precision)


def _dot_tn(a, b, precision=None):
    return lax.dot_general(a, b, (((0,), (0,)), ((), ())), preferred_element_type=F32, precision=precision)


def _rms(x, gain):
    return x * lax.rsqrt(jnp.mean(x * x, axis=-1, keepdims=True) + RMS_EPS) * gain


def _sigmoid(x):
    return 1.0 / (1.0 + jnp.exp(-x))


def _silu(x):
    return x * _sigmoid(x)


def _softplus(x):
    return jnp.maximum(x, 0.0) + jnp.log(1.0 + jnp.exp(-jnp.abs(x)))


def _seg64_sum(x):
    lo = lax.broadcasted_iota(jnp.int32, x.shape, 1) < RWKV_DIM
    s_lo = jnp.sum(jnp.where(lo, x, 0.0), axis=-1, keepdims=True)
    s_hi = jnp.sum(jnp.where(lo, 0.0, x), axis=-1, keepdims=True)
    return jnp.where(lo, s_lo, s_hi)


def _shifted_rows(z, scr_ref, carry_ref, first, shifts):
    tm, width = z.shape
    assert width >= 2 * LANES, width
    prev = carry_ref[...]
    scr_ref[0:HALO, :] = jnp.where(first, jnp.zeros_like(prev), prev)
    scr_ref[HALO:HALO + tm, :] = z
    carry_ref[...] = z[tm - HALO:, :]
    return [scr_ref[pl.ds(HALO - s, tm), :] for s in shifts]


def _rope_kernel(pos_ref, invf_ref, c_ref, sa_ref, sb_ref):
    ang = pos_ref[...] * invf_ref[...]
    lane = lax.broadcasted_iota(jnp.int32, ang.shape, 1)
    cos = jnp.cos(ang)
    sin = jnp.sin(ang)
    half = MLA_ROPE // 2
    c_ref[...] = jnp.where(lane < MLA_NOPE, 1.0, jnp.where(lane < MLA_QK, cos, 0.0))
    sa_ref[...] = jnp.where((lane >= MLA_NOPE) & (lane < MLA_NOPE + half), -sin, 0.0)
    sb_ref[...] = jnp.where((lane >= MLA_NOPE + half) & (lane < MLA_QK), sin, 0.0)


def _rope_tables(positions, tm):
    n = positions.size
    pos = jnp.broadcast_to(positions.reshape(n, 1).astype(F32), (n, LANES))
    inv_freq = ROPE_THETA ** (-jnp.arange(0, MLA_ROPE, 2, dtype=F32) / MLA_ROPE)
    half = MLA_ROPE // 2
    invf = jnp.zeros((1, LANES), F32)
    invf = invf.at[0, MLA_NOPE:MLA_NOPE + half].set(inv_freq).at[0, MLA_NOPE + half:MLA_QK].set(inv_freq)
    spec = pl.BlockSpec((tm, LANES), lambda i: (i, 0))
    out = jax.ShapeDtypeStruct((n, LANES), F32)
    return pl.pallas_call(
        _rope_kernel,
        grid=(n // tm,),
        in_specs=[spec, pl.BlockSpec((1, LANES), lambda i: (0, 0))],
        out_specs=[spec, spec, spec],
        out_shape=[out, out, out],
        name="rope_tables",
    )(pos, invf)


def _ev_in_kernel(x_ref, gain_ref, w_ref, ws_ref, ca_ref, cq_ref, alog_ref, dtb_ref,
                  ya_ref, q_ref, k_ref, v_ref, gate_ref, bg_ref,
                  scr_ref, carry_p_ref, carry_qkv_ref):
    first = pl.program_id(1) == 0
    xn = _rms(x_ref[...], gain_ref[...]).astype(BF16)

    za = _dot_nt(xn, w_ref[0:3 * A_WIDTH, :])
    a_b = za[:, 0:A_WIDTH]
    prod = za[:, A_WIDTH:2 * A_WIDTH] * za[:, 2 * A_WIDTH:3 * A_WIDTH]
    p1, p2 = _shifted_rows(prod, scr_ref.at[:, 0:A_WIDTH], carry_p_ref, first, (1, 2))
    ca = ca_ref[...]
    ya = a_b * (ca[0:1, :] * p2 + ca[1:2, :] * p1 + ca[2:3, :] * prod)
    ya_ref[...] = ya.astype(ya_ref.dtype)

    base = 3 * A_WIDTH
    for g, out_ref in enumerate((q_ref, k_ref, v_ref)):
        cols = slice(base + g * GDN_WIDTH, base + (g + 1) * GDN_WIDTH)
        z = _dot_nt(xn, w_ref[cols, :])
        z1, z2, z3 = _shifted_rows(z, scr_ref, carry_qkv_ref.at[g], first, (1, 2, 3))
        cw = cq_ref[:, g * GDN_WIDTH:(g + 1) * GDN_WIDTH]
        y = _silu(cw[0:1, :] * z3 + cw[1:2, :] * z2 + cw[2:3, :] * z1 + cw[3:4, :] * z)
        for h in range(GDN_HEADS):
            hs = slice(h * GDN_DIM, (h + 1) * GDN_DIM)
            yh = y[:, hs]
            if g < 2:
                yh = yh * lax.rsqrt(jnp.sum(yh * yh, axis=-1, keepdims=True) + L2_EPS)
            if g == 0:
                yh = yh * (GDN_DIM ** -0.5)
            out_ref[:, hs] = yh

    gcols = slice(base + 3 * GDN_WIDTH, base + 4 * GDN_WIDTH)
    gate_ref[...] = _dot_nt(xn, w_ref[gcols, :])

    zs = _dot_nt(xn, ws_ref[...])
    lane = lax.broadcasted_iota(jnp.int32, zs.shape, 1)
    g_log = -jnp.exp(alog_ref[...]) * _softplus(zs + dtb_ref[...])
    bg_ref[...] = jnp.where(lane < GDN_HEADS, _sigmoid(zs), g_log)


def _ev_in(x2, gain, w_main, w_small, conv_a, conv_qkv, alog_l, dtb_l, bsz, tlen, tm):
    n, d = x2.shape
    nt = tlen // tm
    row = lambda b, i: (b * nt + i, 0)
    const = lambda b, i: (0, 0)
    rs = lambda w: pl.BlockSpec((tm, w), row)
    full = lambda a: pl.BlockSpec(a.shape, const)
    outs = [
        jax.ShapeDtypeStruct((n, A_WIDTH), BF16),
        jax.ShapeDtypeStruct((n, GDN_WIDTH), F32),
        jax.ShapeDtypeStruct((n, GDN_WIDTH), F32),
        jax.ShapeDtypeStruct((n, GDN_WIDTH), F32),
        jax.ShapeDtypeStruct((n, GDN_WIDTH), F32),
        jax.ShapeDtypeStruct((n, LANES), F32),
    ]
    return pl.pallas_call(
        _ev_in_kernel,
        grid=(bsz, nt),
        in_specs=[rs(d), full(gain), full(w_main), full(w_small), full(conv_a), full(conv_qkv),
                  full(alog_l), full(dtb_l)],
        out_specs=[rs(A_WIDTH), rs(GDN_WIDTH), rs(GDN_WIDTH), rs(GDN_WIDTH), rs(GDN_WIDTH), rs(LANES)],
        out_shape=outs,
        scratch_shapes=[
            pltpu.VMEM((HALO + tm, GDN_WIDTH), F32),
            pltpu.VMEM((HALO, A_WIDTH), F32),
            pltpu.VMEM((3, HALO, GDN_WIDTH), F32),
        ],
        compiler_params=pltpu.CompilerParams(
            dimension_semantics=("arbitrary", "arbitrary"), vmem_limit_bytes=VMEM_LIMIT),
        name="even_in_proj",
    )(x2, gain, w_main, w_small, conv_a, conv_qkv, alog_l, dtb_l)


def _unit_lower_inverse_many(l_mats, row, col, dot):
    eye = (row == col).astype(F32)
    in_block = (row // GDN_BLOCK) == (col // GDN_BLOCK)
    l_diag = [jnp.where(in_block, m, 0.0) for m in l_mats]
    l_off = [jnp.where(in_block, 0.0, m) for m in l_mats]
    x2 = [dot(m, m) for m in l_diag]
    x4 = [dot(m, m) for m in x2]
    x8 = [dot(m, m) for m in x4]
    d_inv = [eye - m for m in l_diag]
    d_inv = [m + dot(m, x) for m, x in zip(d_inv, x2)]
    d_inv = [m + dot(m, x) for m, x in zip(d_inv, x4)]
    d_inv = [m + dot(m, x) for m, x in zip(d_inv, x8)]
    n1 = [dot(m, x) for m, x in zip(d_inv, l_off)]
    n2 = [dot(m, m) for m in n1]
    outer = [eye - m for m in n1]
    outer = [m + dot(m, x) for m, x in zip(outer, n2)]
    return [dot(m, x) for m, x in zip(outer, d_inv)]


def _gdn_kernel(q_ref, k_ref, v_ref, bg_ref, gate_ref, onorm_ref, o_ref, s_ref, *, nchunk):
    @pl.when(pl.program_id(1) == 0)
    def _():
        s_ref[...] = jnp.zeros_like(s_ref)

    c = GDN_CHUNK
    dk = GDN_DIM
    pairs = range(GDN_HEADS // 2)
    row_c = lax.broadcasted_iota(jnp.int32, (c, c), 0)
    col_c = lax.broadcasted_iota(jnp.int32, (c, c), 1)
    tri = (row_c >= col_c).astype(F32)
    row1 = lax.broadcasted_iota(jnp.int32, (2 * c, 2 * c), 0)
    col1 = lax.broadcasted_iota(jnp.int32, (2 * c, 2 * c), 1)
    same_head = (row1 // c) == (col1 // c)
    lag = (row1 % c) - (col1 % c)
    causal = same_head & (lag >= 0)
    strict = same_head & (lag > 0)
    tri_dup = (lax.broadcasted_iota(jnp.int32, (c, 2 * c), 0) <= lax.broadcasted_iota(jnp.int32, (c, 2 * c), 1) % c).astype(F32)
    first_half = lax.broadcasted_iota(jnp.int32, (1, 2 * c), 1) < c
    head0_rows = (lax.broadcasted_iota(jnp.int32, (2 * c, 1), 0) < c).astype(F32)
    onorm = onorm_ref[...]

    def bdot(a, b):
        return _dot(a.astype(BF16), b.astype(BF16))

    def rows2(a, b):
        return jnp.concatenate([a, b], axis=0)

    items = [(ci, p) for ci in range(nchunk) for p in pairs]
    q_s, k_s, v_s, beta_s, gc_s, gl_s, l_mats, qk = [], [], [], [], [], [], [], []
    for ci in range(nchunk):
        rows = slice(ci * c, (ci + 1) * c)
        bg = bg_ref[rows, :]
        g_cols = _dot(tri, bg, precision=HIGHEST)
        g_rows = _dot_tn(bg, tri_dup, precision=HIGHEST)
        for p in pairs:
            h0, h1 = 2 * p, 2 * p + 1
            s0, s1 = slice(h0 * dk, (h0 + 1) * dk), slice(h1 * dk, (h1 + 1) * dk)
            q_s.append(rows2(q_ref[rows, s0], q_ref[rows, s1]))
            k_s.append(rows2(k_ref[rows, s0], k_ref[rows, s1]))
            v_s.append(rows2(v_ref[rows, s0], v_ref[rows, s1]))
            beta_s.append(rows2(bg[:, h0:h0 + 1], bg[:, h1:h1 + 1]))
            gc0 = g_cols[:, GDN_HEADS + h0:GDN_HEADS + h0 + 1]
            gc1 = g_cols[:, GDN_HEADS + h1:GDN_HEADS + h1 + 1]
            gc_s.append(rows2(gc0, gc1))
            gl_s.append(rows2(jnp.broadcast_to(gc0[c - 1:c, :], (c, 1)), jnp.broadcast_to(gc1[c - 1:c, :], (c, 1))))
            gr = jnp.where(first_half, g_rows[GDN_HEADS + h0:GDN_HEADS + h0 + 1, :], g_rows[GDN_HEADS + h1:GDN_HEADS + h1 + 1, :])
            diff = gc_s[-1] - gr
            decay = jnp.where(causal, jnp.exp(jnp.where(causal, diff, 0.0)), 0.0)
            kb = k_s[-1] * beta_s[-1]
            score = _dot_nt(rows2(kb, q_s[-1]).astype(BF16), k_s[-1].astype(BF16))
            l_mats.append(jnp.where(strict, score[0:2 * c] * decay, 0.0))
            qk.append(score[2 * c:4 * c] * decay)
    t_inv = _unit_lower_inverse_many(l_mats, row1, col1, bdot)
    eg = [jnp.exp(g) for g in gc_s]
    wu = [bdot(t_inv[i], jnp.concatenate([k_s[i] * (beta_s[i] * eg[i]), v_s[i] * beta_s[i]], axis=1)) for i in range(len(items))]
    qy = [bdot(qk[i], wu[i]) for i in range(len(items))]
    gd = []
    for i in range(len(items)):
        k_dec = k_s[i] * jnp.exp(gl_s[i] - gc_s[i])
        wu0 = wu[i] * head0_rows
        wu1 = wu[i] - wu0
        expanded = jnp.concatenate([wu0[:, 0:dk], wu1[:, 0:dk], wu0[:, dk:], wu1[:, dk:]], axis=1)
        gd.append(_dot_tn(k_dec.astype(BF16), expanded.astype(BF16)))

    for i, (ci, p) in enumerate(items):
        rows = slice(ci * c, (ci + 1) * c)
        q_hat = q_s[i] * eg[i] - qy[i][:, 0:dk]
        y0 = qy[i][:, dk:]
        for hh in range(2):
            h = 2 * p + hh
            hs = slice(h * dk, (h + 1) * dk)
            hr = slice(hh * c, (hh + 1) * c)
            state = s_ref[h]
            o = bdot(q_hat[hr], state) + y0[hr]
            g_last = gl_s[i][hh * c:hh * c + 1, :]
            s_ref[h] = state * jnp.exp(g_last) - bdot(gd[i][:, hh * dk:(hh + 1) * dk], state) + gd[i][:, (2 + hh) * dk:(3 + hh) * dk]
            o = _rms(o, onorm) * _silu(gate_ref[rows, hs])
            o_ref[rows, hs] = o.astype(o_ref.dtype)


def _gdn(q, k, v, bg, gate, onorm, bsz, tlen, tt):
    n = q.shape[0]
    nt = tlen // tt
    nchunk = tt // GDN_CHUNK
    row = lambda b, i: (b * nt + i, 0)
    rs = lambda w: pl.BlockSpec((tt, w), row)
    return pl.pallas_call(
        functools.partial(_gdn_kernel, nchunk=nchunk),
        grid=(bsz, nt),
        in_specs=[rs(GDN_WIDTH), rs(GDN_WIDTH), rs(GDN_WIDTH), rs(LANES),
                  rs(GDN_WIDTH), pl.BlockSpec(onorm.shape, lambda b, i: (0, 0))],
        out_specs=rs(GDN_WIDTH),
        out_shape=jax.ShapeDtypeStruct((n, GDN_WIDTH), BF16),
        scratch_shapes=[pltpu.VMEM((GDN_HEADS, GDN_DIM, GDN_DIM), F32)],
        compiler_params=pltpu.CompilerParams(
            dimension_semantics=("arbitrary", "arbitrary"), vmem_limit_bytes=VMEM_LIMIT),
        name="gated_delta_rule",
    )(q, k, v, bg, gate, onorm)


OD_R = 0
OD_K = RWKV_WIDTH
OD_V = 2 * RWKV_WIDTH
OD_WA = 3 * RWKV_WIDTH
OD_G = OD_WA + LANES
OD_X = RWKV_SHIFT_COLS
OD_SMALL = 3 * LANES
OD_CQ = OD_X + LANES
OD_CKV = OD_CQ + MLA_Q_LORA
OD_COLS = OD_CKV + MLA_KV_LORA


def _od_in_kernel(*refs, has_vres):
    (x_ref, gain_ref, w_ref, mu_ref, wa2_ref, w0_ref, a0_ref, g2_ref, kk_ref, ka_ref,
     qan_ref, kvan_ref, wuq_ref, wuk_ref, wuv_ref, qln_ref, kln_ref,
     c_ref, sa_ref, sb_ref) = refs[:20]
    pos = 20
    if has_vres:
        vfirst_ref, v0_ref, v2_ref = refs[pos:pos + 3]
        pos += 3
    (r_out, lw_out, k_out, v_out, kkn_out, b_out, g_out, q_out, kmla_out, vmla_out) = refs[pos:pos + 10]
    scr_ref, carry_ref = refs[pos + 10:]

    first = pl.program_id(1) == 0
    xn = _rms(x_ref[...], gain_ref[...]).astype(BF16)

    def lerped(start, width, slot):
        z = _dot(xn, w_ref[:, start:start + width])
        (zp,) = _shifted_rows(z, scr_ref.at[:, 0:width], carry_ref.at[:, slot:slot + width], first, (1,))
        return z, z + mu_ref[:, start:start + width] * (zp - z)

    small, small_l = lerped(OD_WA, OD_SMALL, OD_WA)
    wa = small_l[:, 0:LANES]
    gd = small_l[:, LANES:2 * LANES]
    zx = small[:, 2 * LANES:3 * LANES]
    zx_l = small_l[:, 2 * LANES:3 * LANES]

    cos_t = c_ref[...]
    sin_a = sa_ref[...]
    sin_b = sb_ref[...]

    def rope(t):
        return t * cos_t + pltpu.roll(t, LANES - MLA_ROPE // 2, 1) * sin_a + pltpu.roll(t, MLA_ROPE // 2, 1) * sin_b

    cq = _dot(xn, w_ref[:, OD_CQ:OD_CQ + MLA_Q_LORA])
    q_all = _dot(_rms(cq, qan_ref[...]).astype(BF16), wuq_ref[...])
    ckv = _dot(xn, w_ref[:, OD_CKV:OD_CKV + MLA_KV_LORA])
    ckv_n = _rms(ckv, kvan_ref[...]).astype(BF16)
    k_all = _dot(ckv_n, wuk_ref[...])
    vmla_out[...] = _dot(ckv_n, wuv_ref[...]).astype(vmla_out.dtype)
    lane = lax.broadcasted_iota(jnp.int32, zx.shape, 1)
    k_rope = jnp.where((lane >= MLA_NOPE) & (lane < MLA_QK), pltpu.roll(zx, MLA_NOPE, 1), 0.0)
    q_ln = qln_ref[...]
    k_ln = kln_ref[...]
    scale = MLA_QK ** -0.5

    def mla_heads(heads):
        for h in heads:
            hs = slice(h * MLA_PAD, (h + 1) * MLA_PAD)
            qh = q_all[:, hs]
            qh = qh * lax.rsqrt(jnp.sum(qh * qh, axis=-1, keepdims=True) * (1.0 / MLA_QK) + RMS_EPS) * q_ln
            q_out[:, hs] = (rope(qh) * scale).astype(q_out.dtype)
            kh = k_all[:, hs] + k_rope
            kh = kh * lax.rsqrt(jnp.sum(kh * kh, axis=-1, keepdims=True) * (1.0 / MLA_QK) + RMS_EPS) * k_ln
            kmla_out[:, hs] = rope(kh).astype(kmla_out.dtype)

    _, r = lerped(OD_R, RWKV_WIDTH, OD_R)
    r_out[...] = r
    mla_heads((0, 1, 2))

    lora_in = jnp.where(lane < RWKV_W_LORA, jnp.tanh(wa), wa).astype(BF16)
    lora = _dot(lora_in, wa2_ref[...])
    w_log = -_softplus(-(w0_ref[...] + lora[:, 0:RWKV_WIDTH])) - 0.5
    lw_out[...] = -jnp.exp(w_log)
    a = _sigmoid(a0_ref[...] + lora[:, RWKV_WIDTH:2 * RWKV_WIDTH])
    mla_heads((3, 4, 5))

    _, k = lerped(OD_K, RWKV_WIDTH, OD_K)
    kx = k * kk_ref[...]
    for p in range(RWKV_PAIRS):
        ps = slice(p * LANES, (p + 1) * LANES)
        kxp = kx[:, ps]
        kkp = kxp * lax.rsqrt(_seg64_sum(kxp * kxp) + L2_EPS)
        kkn_out[:, ps] = kkp
        b_out[:, ps] = kkp * a[:, ps]
    k_out[...] = k * (1.0 + (a - 1.0) * ka_ref[...])
    mla_heads((6, 7))

    _, v = lerped(OD_V, RWKV_WIDTH, OD_V)
    if has_vres:
        mix = _sigmoid(v0_ref[...] + _dot(zx_l.astype(BF16), v2_ref[...]))
        v = v + (vfirst_ref[...] - v) * mix
    v_out[...] = v
    g_out[...] = _dot(_sigmoid(gd).astype(BF16), g2_ref[...])


def _od_in(x2, gain, w_in, mu, wa2, w0, a0, g2, k_k, k_a, qan, kvan, wuq, wuk, wuv, qln, kln,
           tables, vres, bsz, tlen, tm):
    n, d = x2.shape
    nt = tlen // tm
    row = lambda b, i: (b * nt + i, 0)
    const = lambda b, i: (0, 0)
    rs = lambda w: pl.BlockSpec((tm, w), row)
    full = lambda a: pl.BlockSpec(a.shape, const)
    params = [gain, w_in, mu, wa2, w0, a0, g2, k_k, k_a, qan, kvan, wuq, wuk, wuv, qln, kln]
    args = [x2] + params + list(tables)
    in_specs = [rs(d)] + [full(a) for a in params] + [rs(LANES)] * 3
    if vres is not None:
        v_first, v0, v2 = vres
        args += [v_first, v0, v2]
        in_specs += [rs(RWKV_WIDTH), full(v0), full(v2)]
    f32w = lambda w: jax.ShapeDtypeStruct((n, w), F32)
    bfw = lambda w: jax.ShapeDtypeStruct((n, w), BF16)
    outs = [f32w(RWKV_WIDTH)] * 7 + [bfw(MLA_HEADS * MLA_PAD), bfw(MLA_HEADS * MLA_PAD), bfw(MLA_HEADS * MLA_V)]
    out_specs = [rs(RWKV_WIDTH)] * 7 + [rs(MLA_HEADS * MLA_PAD), rs(MLA_HEADS * MLA_PAD), rs(MLA_HEADS * MLA_V)]
    carry_w = RWKV_SHIFT_COLS + LANES
    return pl.pallas_call(
        functools.partial(_od_in_kernel, has_vres=vres is not None),
        grid=(bsz, nt),
        in_specs=in_specs,
        out_specs=out_specs,
        out_shape=outs,
        scratch_shapes=[pltpu.VMEM((HALO + tm, RWKV_WIDTH), F32), pltpu.VMEM((HALO, carry_w), F32)],
        compiler_params=pltpu.CompilerParams(
            dimension_semantics=("arbitrary", "arbitrary"), vmem_limit_bytes=VMEM_LIMIT),
        name="odd_in_proj",
    )(*args)


def _rwkv_kernel(r_ref, lw_ref, k_ref, v_ref, kk_ref, b_ref, g_ref, rk_ref, lnw_ref, lnb_ref,
                 o_ref, h_ref, y_ref, *, nchunk):
    @pl.when(pl.program_id(1) == 0)
    def _():
        h_ref[...] = jnp.zeros_like(h_ref)

    c = RWKV_CHUNK
    d = RWKV_DIM
    lane = lax.broadcasted_iota(jnp.int32, (1, LANES), 1)
    m_lo = (lane < d).astype(F32)
    m_hi = 1.0 - m_lo
    row_c = lax.broadcasted_iota(jnp.int32, (c, c), 0)
    col_c = lax.broadcasted_iota(jnp.int32, (c, c), 1)
    tri = (row_c >= col_c).astype(F32)
    row1 = lax.broadcasted_iota(jnp.int32, (2 * c, 2 * c), 0)
    col1 = lax.broadcasted_iota(jnp.int32, (2 * c, 2 * c), 1)
    eye1 = row1 == col1
    row2 = lax.broadcasted_iota(jnp.int32, (4 * c, 4 * c), 0)
    col2 = lax.broadcasted_iota(jnp.int32, (4 * c, 4 * c), 1)
    same_head = ((row2 // c) % 2) == ((col2 // c) % 2)
    min_lag = jnp.where(row2 < 2 * c, 1, 0)
    keep = same_head & ((row2 % c) - (col2 % c) >= min_lag)

    def bdot(a, b):
        return _dot(a.astype(BF16), b.astype(BF16))

    def stack(x):
        return jnp.concatenate([x * m_lo, x * m_hi], axis=0)

    pairs = range(RWKV_PAIRS)
    ps = [slice(p * LANES, (p + 1) * LANES) for p in pairs]
    zero = jnp.zeros((2 * c, LANES), F32)
    items = [(ci, p) for ci in range(nchunk) for p in pairs]
    n_items = range(len(items))
    qt_p, w_last_p, am_s, v_s, lhs1, rhs1, lhs_t = [], [], [], [], [], [], []
    for ci in range(nchunk):
        rows = slice(ci * c, (ci + 1) * c)
        lw = lw_ref[rows, :]
        logw = _dot(tri, lw, precision=HIGHEST)
        logw_last = logw[c - 1:c, :]
        w_inv = jnp.exp(-logw)
        w_last = jnp.exp(logw_last)
        to_end = jnp.exp(logw_last - logw)
        am = kk_ref[rows, :] * jnp.exp(logw - lw)
        qt = r_ref[rows, :] * jnp.exp(logw)
        bp = b_ref[rows, :] * w_inv
        kp = k_ref[rows, :] * w_inv
        bh = b_ref[rows, :] * to_end
        kh = k_ref[rows, :] * to_end
        v = v_ref[rows, :]
        for s in ps:
            qt_p.append(qt[:, s])
            w_last_p.append(w_last[:, s])
            am_s.append(stack(am[:, s]))
            v_s.append(stack(v[:, s]))
            lhs1.append(jnp.concatenate([am_s[-1], stack(qt[:, s])], axis=0).astype(BF16))
            rhs1.append(jnp.concatenate([bp[:, s], bp[:, s], kp[:, s], kp[:, s]], axis=0).astype(BF16))
            lhs_t.append(jnp.concatenate([stack(bh[:, s]), stack(kh[:, s])], axis=0).astype(BF16))
    score = [jnp.where(keep, _dot_nt(lhs1[i], rhs1[i]), 0.0) for i in n_items]
    t_inv = _unit_lower_inverse_many([sc[0:2 * c, 0:2 * c] for sc in score], row1, col1, bdot)
    r1 = [bdot(score[i][0:2 * c, 2 * c:4 * c], v_s[i]) for i in n_items]
    x = [bdot(t_inv[i], jnp.concatenate([am_s[i], r1[i]], axis=1)) for i in n_items]
    rhs2 = [jnp.concatenate([jnp.concatenate([x[i][:, 0:LANES], -x[i][:, LANES:]], axis=1),
                             jnp.concatenate([zero, v_s[i]], axis=1)], axis=0).astype(BF16) for i in n_items]
    qy = [_dot(score[i][2 * c:4 * c, :].astype(BF16), rhs2[i]) for i in n_items]
    gd = [_dot_tn(lhs_t[i], rhs2[i]) for i in n_items]

    for i, (ci, p) in enumerate(items):
        rows = slice(ci * c, (ci + 1) * c)
        q_hat = qt_p[i] - (qy[i][0:c, 0:LANES] + qy[i][c:2 * c, 0:LANES])
        y0 = qy[i][0:c, LANES:] + qy[i][c:2 * c, LANES:]
        h = h_ref[p]
        y_ref[rows, ps[p]] = bdot(q_hat, h) + y0
        wl_col = jnp.sum(jnp.where(eye1, w_last_p[i], 0.0), axis=1, keepdims=True)
        h_ref[p] = wl_col * h - bdot(gd[i][:, 0:LANES], h) + gd[i][:, LANES:]

    for p in pairs:
        s = slice(p * LANES, (p + 1) * LANES)
        y = y_ref[:, s]
        mu = _seg64_sum(y) * (1.0 / d)
        yc = y - mu
        var = _seg64_sum(yc * yc) * (1.0 / d)
        y = yc * lax.rsqrt(var + RWKV_LN_EPS) * lnw_ref[:, s] + lnb_ref[:, s]
        bonus = _seg64_sum(r_ref[:, s] * k_ref[:, s] * rk_ref[:, s])
        y = (y + bonus * v_ref[:, s]) * g_ref[:, s]
        o_ref[:, s] = y.astype(o_ref.dtype)


def _rwkv(r, lw, k, v, kk, b, g, rk, lnw, lnb, bsz, tlen, tt):
    n, width = r.shape
    nt = tlen // tt
    blk = pl.BlockSpec((tt, width), lambda bb, i: (bb * nt + i, 0))
    par = pl.BlockSpec((1, width), lambda bb, i: (0, 0))
    return pl.pallas_call(
        functools.partial(_rwkv_kernel, nchunk=tt // RWKV_CHUNK),
        grid=(bsz, nt),
        in_specs=[blk] * 7 + [par] * 3,
        out_specs=blk,
        out_shape=jax.ShapeDtypeStruct((n, width), BF16),
        scratch_shapes=[pltpu.VMEM((RWKV_PAIRS, LANES, LANES), F32), pltpu.VMEM((tt, width), F32)],
        compiler_params=pltpu.CompilerParams(
            dimension_semantics=("arbitrary", "arbitrary"), vmem_limit_bytes=VMEM_LIMIT),
        name="rwkv7_chunked",
    )(r, lw, k, v, kk, b, g, rk, lnw, lnb)


def _mla_kernel(q_ref, k_ref, v_ref, o_ref, m_ref, acc_ref, *, tq, tk):
    qi = pl.program_id(2)
    ntile = tk // LANES
    row = lax.broadcasted_iota(jnp.int32, (tq, LANES), 0)
    lane = lax.broadcasted_iota(jnp.int32, (tq, LANES), 1)
    own = [lax.broadcasted_iota(jnp.int32, (tk, LANES), 1) < MLA_V,
           lax.broadcasted_iota(jnp.int32, (tk, LANES), 1) >= MLA_V]
    ones = jnp.ones((tk, LANES), BF16)

    def block(h, q, j, diag_offset):
        hs = slice(h * MLA_PAD, (h + 1) * MLA_PAD)
        start = pl.multiple_of(j * tk, tk)
        s = _dot_nt(q, k_ref[pl.ds(start, tk), hs])
        tiles = [s[:, i * LANES:(i + 1) * LANES] for i in range(ntile)]
        if diag_offset is not None:
            tiles = [jnp.where(lane + (diag_offset + i * LANES) <= row, t, -jnp.inf) for i, t in enumerate(tiles)]
        m_prev = m_ref[h]
        m_tile = tiles[0]
        for t in tiles[1:]:
            m_tile = jnp.maximum(m_tile, t)
        m_new = jnp.maximum(m_prev, jnp.max(m_tile, axis=-1, keepdims=True))
        alpha = jnp.exp(m_prev - m_new)
        p = jnp.concatenate([jnp.exp((t - m_new).astype(BF16)) for t in tiles], axis=1)
        v_ext = jnp.where(own[h], v_ref[pl.ds(start, tk), :], ones)
        acc_ref[h] = alpha * acc_ref[h] + _dot(p, v_ext)
        m_ref[h] = m_new

    ratio = tq // tk
    qs = [q_ref[:, h * MLA_PAD:(h + 1) * MLA_PAD] for h in range(2)]
    for h in range(2):
        m_ref[h] = jnp.full((tq, LANES), -jnp.inf, F32)
        acc_ref[h] = jnp.zeros((tq, LANES), F32)

    def full_blocks(j, count):
        for u in range(count):
            for h in range(2):
                block(h, qs[h], j + u, None)

    n_full = qi * ratio

    def two_blocks(j2, carry):
        full_blocks(2 * j2, 2)
        return carry

    lax.fori_loop(0, n_full // 2, two_blocks, 0)

    @pl.when(n_full % 2 == 1)
    def _():
        full_blocks(n_full - 1, 1)

    for d in range(ratio):
        for h in range(2):
            block(h, qs[h], qi * ratio + d, d * tk)
    outs = [acc_ref[h] / pltpu.roll(acc_ref[h], MLA_V, 1) for h in range(2)]
    o_ref[...] = jnp.where(lane < MLA_V, outs[0], outs[1]).astype(o_ref.dtype)


def _mla(q, k, v, bsz, tlen, tq, tk):
    n = q.shape[0]
    nq = tlen // tq
    hp = MLA_HEADS // 2
    stat = pltpu.VMEM((2, tq, LANES), F32)
    return pl.pallas_call(
        functools.partial(_mla_kernel, tq=tq, tk=tk),
        grid=(bsz, hp, nq),
        in_specs=[pl.BlockSpec((tq, 2 * MLA_PAD), lambda b, h, i: (b * nq + i, h)),
                  pl.BlockSpec((tlen, 2 * MLA_PAD), lambda b, h, i: (b, h)),
                  pl.BlockSpec((tlen, 2 * MLA_V), lambda b, h, i: (b, h))],
        out_specs=pl.BlockSpec((tq, 2 * MLA_V), lambda b, h, i: (b * nq + i, h)),
        out_shape=jax.ShapeDtypeStruct((n, MLA_HEADS * MLA_V), BF16),
        scratch_shapes=[stat, stat],
        compiler_params=pltpu.CompilerParams(
            dimension_semantics=("arbitrary", "arbitrary", "arbitrary"), vmem_limit_bytes=VMEM_LIMIT),
        name="mla_attention",
    )(q, k, v)


def _ffn_kernel(x_ref, ma_ref, mb_ref, woa_ref, wob_ref, gain_ref, wgu_ref, wd_ref, o_ref, h_ref, *, fc):
    d_ff = wd_ref.shape[0]
    x1 = x_ref[...] + _dot(ma_ref[...], woa_ref[...]) + _dot(mb_ref[...], wob_ref[...])
    xn = _rms(x1, gain_ref[...]).astype(BF16)
    for c in range(d_ff // fc):
        gate = _dot(xn, wgu_ref[:, c * fc:(c + 1) * fc])
        up = _dot(xn, wgu_ref[:, d_ff + c * fc:d_ff + (c + 1) * fc])
        h_ref[:, c * fc:(c + 1) * fc] = (_silu(gate) * up).astype(BF16)
    o_ref[...] = x1 + _dot(h_ref[...], wd_ref[...])


def _ffn(x2, mix_a, mix_b, wo_a, wo_b, gain, w_gu, w_down, tm, fc):
    n, d = x2.shape
    d_ff = w_down.shape[0]
    row = lambda i: (i, 0)
    const = lambda i: (0, 0)
    rs = lambda w: pl.BlockSpec((tm, w), row)
    full = lambda a: pl.BlockSpec(a.shape, const, pipeline_mode=pl.Buffered(1))
    return pl.pallas_call(
        functools.partial(_ffn_kernel, fc=fc),
        grid=(n // tm,),
        in_specs=[rs(d), rs(mix_a.shape[1]), rs(mix_b.shape[1]), full(wo_a), full(wo_b), full(gain),
                  full(w_gu), full(w_down)],
        out_specs=rs(d),
        out_shape=jax.ShapeDtypeStruct((n, d), F32),
        scratch_shapes=[pltpu.VMEM((tm, d_ff), BF16)],
        compiler_params=pltpu.CompilerParams(
            dimension_semantics=("arbitrary",), vmem_limit_bytes=VMEM_LIMIT),
        name="outproj_ffn",
    )(x2, mix_a, mix_b, wo_a, wo_b, gain, w_gu, w_down)


def _row(v):
    return v.reshape(1, -1).astype(F32)


def _pad_lanes(v, width, offset=0):
    out = jnp.zeros((1, width), F32)
    return out.at[0, offset:offset + v.shape[-1]].set(v.astype(F32))


def _pad_heads(w, heads, src, dst):
    rows = w.shape[0]
    w = w.reshape(rows, heads, src)
    return jnp.pad(w, ((0, 0), (0, 0), (0, dst - src))).reshape(rows, heads * dst)


def _tile(tlen, want):
    t = min(want, tlen)
    assert tlen % t == 0, (tlen, t)
    return t


def kernel(x, positions, norm_mix, norm_ffn, ffn_w_gu, ffn_w_down, ev_w_in, ev_conv_a, ev_conv_qkv, ev_a_log, ev_dt_bias, ev_out_norm, ev_w_out, od_w_in, od_shift_mu, od_w0, od_w2, od_a0, od_a2, od_g2, od_k_k, od_k_a, od_r_k, od_lnx_w, od_lnx_b, od_vres_w1, od_vres_mu, od_vres_v0, od_vres_v2, od_qa_norm, od_kva_norm, od_w_uq, od_w_ukv, od_q_ln, od_k_ln, od_w_out):
    bsz, tlen, d = x.shape
    n = bsz * tlen
    depth = norm_mix.shape[0]
    tm_proj = _tile(tlen, 256)
    tm_ffn = _tile(tlen, 512)
    tt_gdn = _tile(tlen, 256)
    tt_rwkv = _tile(tlen, 256)
    tq_mla = _tile(tlen, 512)
    tk_mla = _tile(tq_mla, 512)
    fc = 256

    x2 = x.reshape(n, d)
    tables = _rope_tables(positions, tm_proj)
    v_first = None

    for layer in range(depth):
        gain = _row(norm_mix[layer])
        if layer % 2 == 0:
            e = layer // 2
            n_main = 3 * A_WIDTH + 4 * GDN_WIDTH
            w_t = jnp.swapaxes(ev_w_in[e], 0, 1)
            w_main = w_t[:n_main].astype(BF16)
            w_small = jnp.zeros((LANES, d), F32).at[:2 * GDN_HEADS].set(w_t[n_main:]).astype(BF16)
            alog_l = _pad_lanes(ev_a_log[e], LANES, GDN_HEADS)
            dtb_l = _pad_lanes(ev_dt_bias[e], LANES, GDN_HEADS)
            ya, q, k, v, gate, bg = _ev_in(x2, gain, w_main, w_small, ev_conv_a[e].astype(F32),
                                           ev_conv_qkv[e].astype(F32), alog_l, dtb_l, bsz, tlen, tm_proj)
            o = _gdn(q, k, v, bg, gate, _row(ev_out_norm[e]), bsz, tlen, tt_gdn)
            mix_a, mix_b = ya, o
            w_out = ev_w_out[e].astype(BF16)
            wo_a, wo_b = w_out[:A_WIDTH], w_out[A_WIDTH:]
        else:
            o = layer // 2
            has_vres = o > 0
            w_in = od_w_in[o]
            extra = jnp.zeros((d, LANES), F32).at[:, :MLA_ROPE].set(w_in[:, RWKV_SHIFT_COLS + MLA_Q_LORA + MLA_KV_LORA:])
            mu = jnp.zeros((1, OD_COLS), F32).at[0, :RWKV_SHIFT_COLS].set(od_shift_mu[o])
            if has_vres:
                extra = extra.at[:, MLA_ROPE:MLA_ROPE + RWKV_V_LORA].set(od_vres_w1[o - 1])
                mu = mu.at[0, OD_X + MLA_ROPE:OD_X + MLA_ROPE + RWKV_V_LORA].set(od_vres_mu[o - 1])
            n_mla = MLA_Q_LORA + MLA_KV_LORA
            w_packed = jnp.concatenate(
                [w_in[:, :RWKV_SHIFT_COLS], extra, w_in[:, RWKV_SHIFT_COLS:RWKV_SHIFT_COLS + n_mla]], axis=1).astype(BF16)
            wa2 = jnp.zeros((LANES, 2 * RWKV_WIDTH), F32)
            wa2 = wa2.at[:RWKV_W_LORA, :RWKV_WIDTH].set(od_w2[o]).at[RWKV_W_LORA:, RWKV_WIDTH:].set(od_a2[o]).astype(BF16)
            wuq = _pad_heads(od_w_uq[o], MLA_HEADS, MLA_QK, MLA_PAD).astype(BF16)
            w_ukv = od_w_ukv[o].reshape(MLA_KV_LORA, MLA_HEADS, MLA_NOPE + MLA_V)
            wuk = _pad_heads(w_ukv[:, :, :MLA_NOPE].reshape(MLA_KV_LORA, -1), MLA_HEADS, MLA_NOPE, MLA_PAD).astype(BF16)
            wuv = w_ukv[:, :, MLA_NOPE:].reshape(MLA_KV_LORA, -1).astype(BF16)
            vres = None
            if has_vres:
                v2 = jnp.zeros((LANES, RWKV_WIDTH), F32).at[MLA_ROPE:MLA_ROPE + RWKV_V_LORA].set(od_vres_v2[o - 1])
                vres = (v_first, _row(od_vres_v0[o - 1]), v2.astype(BF16))
            (r, lw, k, v, kk, b, g, q_mla, k_mla, v_mla) = _od_in(
                x2, gain, w_packed, mu, wa2, _row(od_w0[o]), _row(od_a0[o]), od_g2[o].astype(BF16),
                _row(od_k_k[o]), _row(od_k_a[o]), _row(od_qa_norm[o]), _row(od_kva_norm[o]), wuq, wuk, wuv,
                _pad_lanes(od_q_ln[o], MLA_PAD), _pad_lanes(od_k_ln[o], MLA_PAD), tables, vres, bsz, tlen, tm_proj)
            if not has_vres:
                v_first = v
            y_rwkv = _rwkv(r, lw, k, v, kk, b, g, _row(od_r_k[o]), _row(od_lnx_w[o]), _row(od_lnx_b[o]),
                           bsz, tlen, tt_rwkv)
            o_mla = _mla(q_mla, k_mla, v_mla, bsz, tlen, tq_mla, tk_mla)
            mix_a, mix_b = y_rwkv, o_mla
            w_out = od_w_out[o].astype(BF16)
            wo_a, wo_b = w_out[:RWKV_WIDTH], w_out[RWKV_WIDTH:]
        x2 = _ffn(x2, mix_a, mix_b, wo_a, wo_b, _row(norm_ffn[layer]), ffn_w_gu[layer].astype(BF16),
                  ffn_w_down[layer].astype(BF16), tm_ffn, fc)
    return x2.reshape(bsz, tlen, d)
```

```python
import functools

import jax
import jax.numpy as jnp
from jax import lax
from jax.experimental import pallas as pl
from jax.experimental.pallas import tpu as pltpu

F32 = jnp.float32
BF16 = jnp.bfloat16
HIGHEST = lax.Precision.HIGHEST

LANES = 128
SUBLANES = 8
VMEM_LIMIT = 56 * 1024 * 1024

RMS_EPS = 1e-6
L2_EPS = 1e-6
RWKV_LN_EPS = 64e-5
ROPE_THETA = 10000.0

A_WIDTH = 256
GDN_HEADS = 6
GDN_DIM = 128
GDN_WIDTH = GDN_HEADS * GDN_DIM
GDN_CHUNK = 64
GDN_BLOCK = 16

RWKV_HEADS = 8
RWKV_DIM = 64
RWKV_WIDTH = RWKV_HEADS * RWKV_DIM
RWKV_PAIRS = RWKV_WIDTH // LANES
RWKV_CHUNK = 64
RWKV_W_LORA = 64
RWKV_A_LORA = 64
RWKV_V_LORA = 32
RWKV_G_LORA = 128
RWKV_SHIFT_COLS = 3 * RWKV_WIDTH + RWKV_W_LORA + RWKV_A_LORA + RWKV_G_LORA

MLA_HEADS = 8
MLA_NOPE = 64
MLA_ROPE = 32
MLA_V = 64
MLA_QK = MLA_NOPE + MLA_ROPE
MLA_Q_LORA = 512
MLA_KV_LORA = 256
MLA_PAD = 128

HALO = SUBLANES


def _dot(a, b, precision=None):
    return jnp.dot(a, b, preferred_element_type=F32, precision=precision)


def _dot_nt(a, b, precision=None):
    return lax.dot_general(a, b, (((1,), (1,)), ((), ())), preferred_element_type=F32, precision=precision)


def _dot_tn(a, b, precision=None):
    return lax.dot_general(a, b, (((0,), (0,)), ((), ())), preferred_element_type=F32, precision=precision)


def _rms(x, gain):
    return x * lax.rsqrt(jnp.mean(x * x, axis=-1, keepdims=True) + RMS_EPS) * gain


def _sigmoid(x):
    return 1.0 / (1.0 + jnp.exp(-x))


def _silu(x):
    return x * _sigmoid(x)


def _softplus(x):
    return jnp.maximum(x, 0.0) + jnp.log(1.0 + jnp.exp(-jnp.abs(x)))


def _seg64_sum(x):
    lo = lax.broadcasted_iota(jnp.int32, x.shape, 1) < RWKV_DIM
    s_lo = jnp.sum(jnp.where(lo, x, 0.0), axis=-1, keepdims=True)
    s_hi = jnp.sum(jnp.where(lo, 0.0, x), axis=-1, keepdims=True)
    return jnp.where(lo, s_lo, s_hi)


def _shifted_rows(z, scr_ref, carry_ref, first, shifts):
    tm, width = z.shape
    assert width >= 2 * LANES, width
    prev = carry_ref[...]
    scr_ref[0:HALO, :] = jnp.where(first, jnp.zeros_like(prev), prev)
    scr_ref[HALO:HALO + tm, :] = z
    carry_ref[...] = z[tm - HALO:, :]
    return [scr_ref[pl.ds(HALO - s, tm), :] for s in shifts]


def _rope_kernel(pos_ref, invf_ref, c_ref, sa_ref, sb_ref):
    ang = pos_ref[...] * invf_ref[...]
    lane = lax.broadcasted_iota(jnp.int32, ang.shape, 1)
    cos = jnp.cos(ang)
    sin = jnp.sin(ang)
    half = MLA_ROPE // 2
    c_ref[...] = jnp.where(lane < MLA_NOPE, 1.0, jnp.where(lane < MLA_QK, cos, 0.0))
    sa_ref[...] = jnp.where((lane >= MLA_NOPE) & (lane < MLA_NOPE + half), -sin, 0.0)
    sb_ref[...] = jnp.where((lane >= MLA_NOPE + half) & (lane < MLA_QK), sin, 0.0)


def _rope_tables(positions, tm):
    n = positions.size
    pos = jnp.broadcast_to(positions.reshape(n, 1).astype(F32), (n, LANES))
    inv_freq = ROPE_THETA ** (-jnp.arange(0, MLA_ROPE, 2, dtype=F32) / MLA_ROPE)
    half = MLA_ROPE // 2
    invf = jnp.zeros((1, LANES), F32)
    invf = invf.at[0, MLA_NOPE:MLA_NOPE + half].set(inv_freq).at[0, MLA_NOPE + half:MLA_QK].set(inv_freq)
    spec = pl.BlockSpec((tm, LANES), lambda i: (i, 0))
    out = jax.ShapeDtypeStruct((n, LANES), F32)
    return pl.pallas_call(
        _rope_kernel,
        grid=(n // tm,),
        in_specs=[spec, pl.BlockSpec((1, LANES), lambda i: (0, 0))],
        out_specs=[spec, spec, spec],
        out_shape=[out, out, out],
        name="rope_tables",
    )(pos, invf)


def _transpose_cast_kernel(w_ref, o_ref):
    o_ref[...] = w_ref[...].T.astype(o_ref.dtype)


def _transpose_cast(w_t, rows, tn=256):
    d = w_t.shape[1]
    assert rows % tn == 0, (rows, tn)
    return pl.pallas_call(
        _transpose_cast_kernel,
        grid=(rows // tn,),
        in_specs=[pl.BlockSpec((tn, d), lambda i: (i, 0))],
        out_specs=pl.BlockSpec((d, tn), lambda i: (0, i)),
        out_shape=jax.ShapeDtypeStruct((d, rows), BF16),
        name="weight_transpose",
    )(w_t)


def _ev_in_kernel(x_ref, gain_ref, w_ref, ws_ref, ca_ref, cq_ref, alog_ref, dtb_ref,
                  ya_ref, q_ref, k_ref, v_ref, gate_ref, bg_ref,
                  scr_ref, carry_p_ref, carry_qkv_ref):
    first = pl.program_id(1) == 0
    xn = _rms(x_ref[...], gain_ref[...]).astype(BF16)

    za = _dot(xn, w_ref[:, 0:3 * A_WIDTH])
    a_b = za[:, 0:A_WIDTH]
    prod = za[:, A_WIDTH:2 * A_WIDTH] * za[:, 2 * A_WIDTH:3 * A_WIDTH]
    p1, p2 = _shifted_rows(prod, scr_ref.at[:, 0:A_WIDTH], carry_p_ref, first, (1, 2))
    ca = ca_ref[...]
    ya = a_b * (ca[0:1, :] * p2 + ca[1:2, :] * p1 + ca[2:3, :] * prod)
    ya_ref[...] = ya.astype(ya_ref.dtype)

    base = 3 * A_WIDTH
    for g, out_ref in enumerate((q_ref, k_ref, v_ref)):
        cols = slice(base + g * GDN_WIDTH, base + (g + 1) * GDN_WIDTH)
        z = _dot(xn, w_ref[:, cols])
        z1, z2, z3 = _shifted_rows(z, scr_ref, carry_qkv_ref.at[g], first, (1, 2, 3))
        cw = cq_ref[:, g * GDN_WIDTH:(g + 1) * GDN_WIDTH]
        y = _silu(cw[0:1, :] * z3 + cw[1:2, :] * z2 + cw[2:3, :] * z1 + cw[3:4, :] * z)
        for h in range(GDN_HEADS):
            hs = slice(h * GDN_DIM, (h + 1) * GDN_DIM)
            yh = y[:, hs]
            if g < 2:
                yh = yh * lax.rsqrt(jnp.sum(yh * yh, axis=-1, keepdims=True) + L2_EPS)
            if g == 0:
                yh = yh * (GDN_DIM ** -0.5)
            out_ref[:, hs] = yh

    gcols = slice(base + 3 * GDN_WIDTH, base + 4 * GDN_WIDTH)
    gate_ref[...] = _dot(xn, w_ref[:, gcols])

    zs = _dot(xn, ws_ref[...])
    lane = lax.broadcasted_iota(jnp.int32, zs.shape, 1)
    g_log = -jnp.exp(alog_ref[...]) * _softplus(zs + dtb_ref[...])
    bg_ref[...] = jnp.where(lane < GDN_HEADS, _sigmoid(zs), g_log)


def _ev_in(x2, gain, w_main, w_small, conv_a, conv_qkv, alog_l, dtb_l, bsz, tlen, tm):
    n, d = x2.shape
    nt = tlen // tm
    row = lambda b, i: (b * nt + i, 0)
    const = lambda b, i: (0, 0)
    rs = lambda w: pl.BlockSpec((tm, w), row)
    full = lambda a: pl.BlockSpec(a.shape, const)
    outs = [
        jax.ShapeDtypeStruct((n, A_WIDTH), BF16),
        jax.ShapeDtypeStruct((n, GDN_WIDTH), F32),
        jax.ShapeDtypeStruct((n, GDN_WIDTH), F32),
        jax.ShapeDtypeStruct((n, GDN_WIDTH), F32),
        jax.ShapeDtypeStruct((n, GDN_WIDTH), F32),
        jax.ShapeDtypeStruct((n, LANES), F32),
    ]
    return pl.pallas_call(
        _ev_in_kernel,
        grid=(bsz, nt),
        in_specs=[rs(d), full(gain), full(w_main), full(w_small), full(conv_a), full(conv_qkv),
                  full(alog_l), full(dtb_l)],
        out_specs=[rs(A_WIDTH), rs(GDN_WIDTH), rs(GDN_WIDTH), rs(GDN_WIDTH), rs(GDN_WIDTH), rs(LANES)],
        out_shape=outs,
        scratch_shapes=[
            pltpu.VMEM((HALO + tm, GDN_WIDTH), F32),
            pltpu.VMEM((HALO, A_WIDTH), F32),
            pltpu.VMEM((3, HALO, GDN_WIDTH), F32),
        ],
        compiler_params=pltpu.CompilerParams(
            dimension_semantics=("arbitrary", "arbitrary"), vmem_limit_bytes=VMEM_LIMIT),
        name="even_in_proj",
    )(x2, gain, w_main, w_small, conv_a, conv_qkv, alog_l, dtb_l)


def _unit_lower_inverse_many(l_mats, row, col, dot):
    eye = (row == col).astype(F32)
    in_block = (row // GDN_BLOCK) == (col // GDN_BLOCK)
    l_diag = [jnp.where(in_block, m, 0.0) for m in l_mats]
    l_off = [jnp.where(in_block, 0.0, m) for m in l_mats]
    x2 = [dot(m, m) for m in l_diag]
    x4 = [dot(m, m) for m in x2]
    x8 = [dot(m, m) for m in x4]
    d_inv = [eye - m for m in l_diag]
    d_inv = [m + dot(m, x) for m, x in zip(d_inv, x2)]
    d_inv = [m + dot(m, x) for m, x in zip(d_inv, x4)]
    d_inv = [m + dot(m, x) for m, x in zip(d_inv, x8)]
    n1 = [dot(m, x) for m, x in zip(d_inv, l_off)]
    n2 = [dot(m, m) for m in n1]
    outer = [eye - m for m in n1]
    outer = [m + dot(m, x) for m, x in zip(outer, n2)]
    return [dot(m, x) for m, x in zip(outer, d_inv)]


def _gdn_kernel(q_ref, k_ref, v_ref, bg_ref, gate_ref, onorm_ref, o_ref, s_ref, *, nchunk):
    @pl.when(pl.program_id(1) == 0)
    def _():
        s_ref[...] = jnp.zeros_like(s_ref)

    c = GDN_CHUNK
    dk = GDN_DIM
    pairs = range(GDN_HEADS // 2)
    row_c = lax.broadcasted_iota(jnp.int32, (c, c), 0)
    col_c = lax.broadcasted_iota(jnp.int32, (c, c), 1)
    tri = (row_c >= col_c).astype(F32)
    row1 = lax.broadcasted_iota(jnp.int32, (2 * c, 2 * c), 0)
    col1 = lax.broadcasted_iota(jnp.int32, (2 * c, 2 * c), 1)
    same_head = (row1 // c) == (col1 // c)
    lag = (row1 % c) - (col1 % c)
    causal = same_head & (lag >= 0)
    strict = same_head & (lag > 0)
    tri_dup = (lax.broadcasted_iota(jnp.int32, (c, 2 * c), 0) <= lax.broadcasted_iota(jnp.int32, (c, 2 * c), 1) % c).astype(F32)
    first_half = lax.broadcasted_iota(jnp.int32, (1, 2 * c), 1) < c
    head0_rows = (lax.broadcasted_iota(jnp.int32, (2 * c, 1), 0) < c).astype(F32)
    onorm = onorm_ref[...]

    def bdot(a, b):
        return _dot(a.astype(BF16), b.astype(BF16))

    def rows2(a, b):
        return jnp.concatenate([a, b], axis=0)

    items = [(ci, p) for ci in range(nchunk) for p in pairs]
    q_s, k_s, v_s, beta_s, gc_s, gl_s, l_mats, qk = [], [], [], [], [], [], [], []
    for ci in range(nchunk):
        rows = slice(ci * c, (ci + 1) * c)
        bg = bg_ref[rows, :]
        g_cols = _dot(tri, bg, precision=HIGHEST)
        g_rows = _dot_tn(bg, tri_dup, precision=HIGHEST)
        for p in pairs:
            h0, h1 = 2 * p, 2 * p + 1
            s0, s1 = slice(h0 * dk, (h0 + 1) * dk), slice(h1 * dk, (h1 + 1) * dk)
            q_s.append(rows2(q_ref[rows, s0], q_ref[rows, s1]))
            k_s.append(rows2(k_ref[rows, s0], k_ref[rows, s1]))
            v_s.append(rows2(v_ref[rows, s0], v_ref[rows, s1]))
            beta_s.append(rows2(bg[:, h0:h0 + 1], bg[:, h1:h1 + 1]))
            gc0 = g_cols[:, GDN_HEADS + h0:GDN_HEADS + h0 + 1]
            gc1 = g_cols[:, GDN_HEADS + h1:GDN_HEADS + h1 + 1]
            gc_s.append(rows2(gc0, gc1))
            gl_s.append(rows2(jnp.broadcast_to(gc0[c - 1:c, :], (c, 1)), jnp.broadcast_to(gc1[c - 1:c, :], (c, 1))))
            gr = jnp.where(first_half, g_rows[GDN_HEADS + h0:GDN_HEADS + h0 + 1, :], g_rows[GDN_HEADS + h1:GDN_HEADS + h1 + 1, :])
            diff = gc_s[-1] - gr
            decay = jnp.where(causal, jnp.exp(jnp.where(causal, diff, 0.0)), 0.0)
            kb = k_s[-1] * beta_s[-1]
            score = _dot_nt(rows2(kb, q_s[-1]).astype(BF16), k_s[-1].astype(BF16))
            l_mats.append(jnp.where(strict, score[0:2 * c] * decay, 0.0))
            qk.append(score[2 * c:4 * c] * decay)
    t_inv = _unit_lower_inverse_many(l_mats, row1, col1, bdot)
    eg = [jnp.exp(g) for g in gc_s]
    wu = [bdot(t_inv[i], jnp.concatenate([k_s[i] * (beta_s[i] * eg[i]), v_s[i] * beta_s[i]], axis=1)) for i in range(len(items))]
    qy = [bdot(qk[i], wu[i]) for i in range(len(items))]
    gd = []
    for i in range(len(items)):
        k_dec = k_s[i] * jnp.exp(gl_s[i] - gc_s[i])
        wu0 = wu[i] * head0_rows
        wu1 = wu[i] - wu0
        expanded = jnp.concatenate([wu0[:, 0:dk], wu1[:, 0:dk], wu0[:, dk:], wu1[:, dk:]], axis=1)
        gd.append(_dot_tn(k_dec.astype(BF16), expanded.astype(BF16)))

    for i, (ci, p) in enumerate(items):
        rows = slice(ci * c, (ci + 1) * c)
        q_hat = q_s[i] * eg[i] - qy[i][:, 0:dk]
        y0 = qy[i][:, dk:]
        for hh in range(2):
            h = 2 * p + hh
            hs = slice(h * dk, (h + 1) * dk)
            hr = slice(hh * c, (hh + 1) * c)
            state = s_ref[h]
            o = bdot(q_hat[hr], state) + y0[hr]
            g_last = gl_s[i][hh * c:hh * c + 1, :]
            s_ref[h] = state * jnp.exp(g_last) - bdot(gd[i][:, hh * dk:(hh + 1) * dk], state) + gd[i][:, (2 + hh) * dk:(3 + hh) * dk]
            o = _rms(o, onorm) * _silu(gate_ref[rows, hs])
            o_ref[rows, hs] = o.astype(o_ref.dtype)


def _gdn(q, k, v, bg, gate, onorm, bsz, tlen, tt):
    n = q.shape[0]
    nt = tlen // tt
    nchunk = tt // GDN_CHUNK
    row = lambda b, i: (b * nt + i, 0)
    rs = lambda w: pl.BlockSpec((tt, w), row)
    return pl.pallas_call(
        functools.partial(_gdn_kernel, nchunk=nchunk),
        grid=(bsz, nt),
        in_specs=[rs(GDN_WIDTH), rs(GDN_WIDTH), rs(GDN_WIDTH), rs(LANES),
                  rs(GDN_WIDTH), pl.BlockSpec(onorm.shape, lambda b, i: (0, 0))],
        out_specs=rs(GDN_WIDTH),
        out_shape=jax.ShapeDtypeStruct((n, GDN_WIDTH), BF16),
        scratch_shapes=[pltpu.VMEM((GDN_HEADS, GDN_DIM, GDN_DIM), F32)],
        compiler_params=pltpu.CompilerParams(
            dimension_semantics=("arbitrary", "arbitrary"), vmem_limit_bytes=VMEM_LIMIT),
        name="gated_delta_rule",
    )(q, k, v, bg, gate, onorm)


OD_R = 0
OD_K = RWKV_WIDTH
OD_V = 2 * RWKV_WIDTH
OD_WA = 3 * RWKV_WIDTH
OD_G = OD_WA + LANES
OD_X = RWKV_SHIFT_COLS
OD_SMALL = 3 * LANES
OD_CQ = OD_X + LANES
OD_CKV = OD_CQ + MLA_Q_LORA
OD_COLS = OD_CKV + MLA_KV_LORA


def _od_in_kernel(*refs, has_vres):
    (x_ref, gain_ref, w_ref, mu_ref, wa2_ref, w0_ref, a0_ref, g2_ref, kk_ref, ka_ref,
     qan_ref, kvan_ref, wuq_ref, wuk_ref, wuv_ref, qln_ref, kln_ref,
     c_ref, sa_ref, sb_ref) = refs[:20]
    pos = 20
    if has_vres:
        vfirst_ref, v0_ref, v2_ref = refs[pos:pos + 3]
        pos += 3
    (r_out, lw_out, k_out, v_out, kkn_out, b_out, g_out, q_out, kmla_out, vmla_out) = refs[pos:pos + 10]
    scr_ref, carry_ref = refs[pos + 10:]

    first = pl.program_id(1) == 0
    xn = _rms(x_ref[...], gain_ref[...]).astype(BF16)

    def lerped(start, width, slot):
        z = _dot(xn, w_ref[:, start:start + width])
        (zp,) = _shifted_rows(z, scr_ref.at[:, 0:width], carry_ref.at[:, slot:slot + width], first, (1,))
        return z, z + mu_ref[:, start:start + width] * (zp - z)

    small, small_l = lerped(OD_WA, OD_SMALL, OD_WA)
    wa = small_l[:, 0:LANES]
    gd = small_l[:, LANES:2 * LANES]
    zx = small[:, 2 * LANES:3 * LANES]
    zx_l = small_l[:, 2 * LANES:3 * LANES]

    cos_t = c_ref[...]
    sin_a = sa_ref[...]
    sin_b = sb_ref[...]

    def rope(t):
        return t * cos_t + pltpu.roll(t, LANES - MLA_ROPE // 2, 1) * sin_a + pltpu.roll(t, MLA_ROPE // 2, 1) * sin_b

    cq = _dot(xn, w_ref[:, OD_CQ:OD_CQ + MLA_Q_LORA])
    q_all = _dot(_rms(cq, qan_ref[...]).astype(BF16), wuq_ref[...])
    ckv = _dot(xn, w_ref[:, OD_CKV:OD_CKV + MLA_KV_LORA])
    ckv_n = _rms(ckv, kvan_ref[...]).astype(BF16)
    k_all = _dot(ckv_n, wuk_ref[...])
    vmla_out[...] = _dot(ckv_n, wuv_ref[...]).astype(vmla_out.dtype)
    lane = lax.broadcasted_iota(jnp.int32, zx.shape, 1)
    k_rope = jnp.where((lane >= MLA_NOPE) & (lane < MLA_QK), pltpu.roll(zx, MLA_NOPE, 1), 0.0)
    q_ln = qln_ref[...]
    k_ln = kln_ref[...]
    scale = MLA_QK ** -0.5

    def mla_heads(heads):
        for h in heads:
            hs = slice(h * MLA_PAD, (h + 1) * MLA_PAD)
            qh = q_all[:, hs]
            qh = qh * lax.rsqrt(jnp.sum(qh * qh, axis=-1, keepdims=True) * (1.0 / MLA_QK) + RMS_EPS) * q_ln
            q_out[:, hs] = (rope(qh) * scale).astype(q_out.dtype)
            kh = k_all[:, hs] + k_rope
            kh = kh * lax.rsqrt(jnp.sum(kh * kh, axis=-1, keepdims=True) * (1.0 / MLA_QK) + RMS_EPS) * k_ln
            kmla_out[:, hs] = rope(kh).astype(kmla_out.dtype)

    _, r = lerped(OD_R, RWKV_WIDTH, OD_R)
    r_out[...] = r
    mla_heads((0, 1, 2))

    lora_in = jnp.where(lane < RWKV_W_LORA, jnp.tanh(wa), wa).astype(BF16)
    lora = _dot(lora_in, wa2_ref[...])
    w_log = -_softplus(-(w0_ref[...] + lora[:, 0:RWKV_WIDTH])) - 0.5
    lw_out[...] = -jnp.exp(w_log)
    a = _sigmoid(a0_ref[...] + lora[:, RWKV_WIDTH:2 * RWKV_WIDTH])
    mla_heads((3, 4, 5))

    _, k = lerped(OD_K, RWKV_WIDTH, OD_K)
    kx = k * kk_ref[...]
    for p in range(RWKV_PAIRS):
        ps = slice(p * LANES, (p + 1) * LANES)
        kxp = kx[:, ps]
        kkp = kxp * lax.rsqrt(_seg64_sum(kxp * kxp) + L2_EPS)
        kkn_out[:, ps] = kkp
        b_out[:, ps] = kkp * a[:, ps]
    k_out[...] = k * (1.0 + (a - 1.0) * ka_ref[...])
    mla_heads((6, 7))

    _, v = lerped(OD_V, RWKV_WIDTH, OD_V)
    if has_vres:
        mix = _sigmoid(v0_ref[...] + _dot(zx_l.astype(BF16), v2_ref[...]))
        v = v + (vfirst_ref[...] - v) * mix
    v_out[...] = v
    g_out[...] = _dot(_sigmoid(gd).astype(BF16), g2_ref[...])


def _od_in(x2, gain, w_in, mu, wa2, w0, a0, g2, k_k, k_a, qan, kvan, wuq, wuk, wuv, qln, kln,
           tables, vres, bsz, tlen, tm):
    n, d = x2.shape
    nt = tlen // tm
    row = lambda b, i: (b * nt + i, 0)
    const = lambda b, i: (0, 0)
    rs = lambda w: pl.BlockSpec((tm, w), row)
    full = lambda a: pl.BlockSpec(a.shape, const)
    params = [gain, w_in, mu, wa2, w0, a0, g2, k_k, k_a, qan, kvan, wuq, wuk, wuv, qln, kln]
    args = [x2] + params + list(tables)
    in_specs = [rs(d)] + [full(a) for a in params] + [rs(LANES)] * 3
    if vres is not None:
        v_first, v0, v2 = vres
        args += [v_first, v0, v2]
        in_specs += [rs(RWKV_WIDTH), full(v0), full(v2)]
    f32w = lambda w: jax.ShapeDtypeStruct((n, w), F32)
    bfw = lambda w: jax.ShapeDtypeStruct((n, w), BF16)
    outs = [f32w(RWKV_WIDTH)] * 7 + [bfw(MLA_HEADS * MLA_PAD), bfw(MLA_HEADS * MLA_PAD), bfw(MLA_HEADS * MLA_V)]
    out_specs = [rs(RWKV_WIDTH)] * 7 + [rs(MLA_HEADS * MLA_PAD), rs(MLA_HEADS * MLA_PAD), rs(MLA_HEADS * MLA_V)]
    carry_w = RWKV_SHIFT_COLS + LANES
    return pl.pallas_call(
        functools.partial(_od_in_kernel, has_vres=vres is not None),
        grid=(bsz, nt),
        in_specs=in_specs,
        out_specs=out_specs,
        out_shape=outs,
        scratch_shapes=[pltpu.VMEM((HALO + tm, RWKV_WIDTH), F32), pltpu.VMEM((HALO, carry_w), F32)],
        compiler_params=pltpu.CompilerParams(
            dimension_semantics=("arbitrary", "arbitrary"), vmem_limit_bytes=VMEM_LIMIT),
        name="odd_in_proj",
    )(*args)


def _rwkv_kernel(r_ref, lw_ref, k_ref, v_ref, kk_ref, b_ref, g_ref, rk_ref, lnw_ref, lnb_ref,
                 o_ref, h_ref, y_ref, *, nchunk):
    @pl.when(pl.program_id(1) == 0)
    def _():
        h_ref[...] = jnp.zeros_like(h_ref)

    c = RWKV_CHUNK
    d = RWKV_DIM
    lane = lax.broadcasted_iota(jnp.int32, (1, LANES), 1)
    m_lo = (lane < d).astype(F32)
    m_hi = 1.0 - m_lo
    row_c = lax.broadcasted_iota(jnp.int32, (c, c), 0)
    col_c = lax.broadcasted_iota(jnp.int32, (c, c), 1)
    tri = (row_c >= col_c).astype(F32)
    row1 = lax.broadcasted_iota(jnp.int32, (2 * c, 2 * c), 0)
    col1 = lax.broadcasted_iota(jnp.int32, (2 * c, 2 * c), 1)
    eye1 = row1 == col1
    row2 = lax.broadcasted_iota(jnp.int32, (4 * c, 4 * c), 0)
    col2 = lax.broadcasted_iota(jnp.int32, (4 * c, 4 * c), 1)
    same_head = ((row2 // c) % 2) == ((col2 // c) % 2)
    min_lag = jnp.where(row2 < 2 * c, 1, 0)
    keep = same_head & ((row2 % c) - (col2 % c) >= min_lag)

    def bdot(a, b):
        return _dot(a.astype(BF16), b.astype(BF16))

    def stack(x):
        return jnp.concatenate([x * m_lo, x * m_hi], axis=0)

    pairs = range(RWKV_PAIRS)
    ps = [slice(p * LANES, (p + 1) * LANES) for p in pairs]
    zero = jnp.zeros((2 * c, LANES), F32)
    items = [(ci, p) for ci in range(nchunk) for p in pairs]
    n_items = range(len(items))
    qt_p, w_last_p, am_s, v_s, lhs1, rhs1, lhs_t = [], [], [], [], [], [], []
    for ci in range(nchunk):
        rows = slice(ci * c, (ci + 1) * c)
        lw = lw_ref[rows, :]
        logw = _dot(tri, lw, precision=HIGHEST)
        logw_last = logw[c - 1:c, :]
        w_inv = jnp.exp(-logw)
        w_last = jnp.exp(logw_last)
        to_end = jnp.exp(logw_last - logw)
        am = kk_ref[rows, :] * jnp.exp(logw - lw)
        qt = r_ref[rows, :] * jnp.exp(logw)
        bp = b_ref[rows, :] * w_inv
        kp = k_ref[rows, :] * w_inv
        bh = b_ref[rows, :] * to_end
        kh = k_ref[rows, :] * to_end
        v = v_ref[rows, :]
        for s in ps:
            qt_p.append(qt[:, s])
            w_last_p.append(w_last[:, s])
            am_s.append(stack(am[:, s]))
            v_s.append(stack(v[:, s]))
            lhs1.append(jnp.concatenate([am_s[-1], stack(qt[:, s])], axis=0).astype(BF16))
            rhs1.append(jnp.concatenate([bp[:, s], bp[:, s], kp[:, s], kp[:, s]], axis=0).astype(BF16))
            lhs_t.append(jnp.concatenate([stack(bh[:, s]), stack(kh[:, s])], axis=0).astype(BF16))
    score = [jnp.where(keep, _dot_nt(lhs1[i], rhs1[i]), 0.0) for i in n_items]
    t_inv = _unit_lower_inverse_many([sc[0:2 * c, 0:2 * c] for sc in score], row1, col1, bdot)
    r1 = [bdot(score[i][0:2 * c, 2 * c:4 * c], v_s[i]) for i in n_items]
    x = [bdot(t_inv[i], jnp.concatenate([am_s[i], r1[i]], axis=1)) for i in n_items]
    rhs2 = [jnp.concatenate([jnp.concatenate([x[i][:, 0:LANES], -x[i][:, LANES:]], axis=1),
                             jnp.concatenate([zero, v_s[i]], axis=1)], axis=0).astype(BF16) for i in n_items]
    qy = [_dot(score[i][2 * c:4 * c, :].astype(BF16), rhs2[i]) for i in n_items]
    gd = [_dot_tn(lhs_t[i], rhs2[i]) for i in n_items]

    for i, (ci, p) in enumerate(items):
        rows = slice(ci * c, (ci + 1) * c)
        q_hat = qt_p[i] - (qy[i][0:c, 0:LANES] + qy[i][c:2 * c, 0:LANES])
        y0 = qy[i][0:c, LANES:] + qy[i][c:2 * c, LANES:]
        h = h_ref[p]
        y_ref[rows, ps[p]] = bdot(q_hat, h) + y0
        wl_col = jnp.sum(jnp.where(eye1, w_last_p[i], 0.0), axis=1, keepdims=True)
        h_ref[p] = wl_col * h - bdot(gd[i][:, 0:LANES], h) + gd[i][:, LANES:]

    for p in pairs:
        s = slice(p * LANES, (p + 1) * LANES)
        y = y_ref[:, s]
        mu = _seg64_sum(y) * (1.0 / d)
        yc = y - mu
        var = _seg64_sum(yc * yc) * (1.0 / d)
        y = yc * lax.rsqrt(var + RWKV_LN_EPS) * lnw_ref[:, s] + lnb_ref[:, s]
        bonus = _seg64_sum(r_ref[:, s] * k_ref[:, s] * rk_ref[:, s])
        y = (y + bonus * v_ref[:, s]) * g_ref[:, s]
        o_ref[:, s] = y.astype(o_ref.dtype)


def _rwkv(r, lw, k, v, kk, b, g, rk, lnw, lnb, bsz, tlen, tt):
    n, width = r.shape
    nt = tlen // tt
    blk = pl.BlockSpec((tt, width), lambda bb, i: (bb * nt + i, 0))
    par = pl.BlockSpec((1, width), lambda bb, i: (0, 0))
    return pl.pallas_call(
        functools.partial(_rwkv_kernel, nchunk=tt // RWKV_CHUNK),
        grid=(bsz, nt),
        in_specs=[blk] * 7 + [par] * 3,
        out_specs=blk,
        out_shape=jax.ShapeDtypeStruct((n, width), BF16),
        scratch_shapes=[pltpu.VMEM((RWKV_PAIRS, LANES, LANES), F32), pltpu.VMEM((tt, width), F32)],
        compiler_params=pltpu.CompilerParams(
            dimension_semantics=("arbitrary", "arbitrary"), vmem_limit_bytes=VMEM_LIMIT),
        name="rwkv7_chunked",
    )(r, lw, k, v, kk, b, g, rk, lnw, lnb)


def _mla_kernel(q_ref, k_ref, v_ref, o_ref, m_ref, acc_ref, *, tq, tk):
    qi = pl.program_id(2)
    ntile = tk // LANES
    row = lax.broadcasted_iota(jnp.int32, (tq, LANES), 0)
    lane = lax.broadcasted_iota(jnp.int32, (tq, LANES), 1)
    own = [lax.broadcasted_iota(jnp.int32, (tk, LANES), 1) < MLA_V,
           lax.broadcasted_iota(jnp.int32, (tk, LANES), 1) >= MLA_V]
    ones = jnp.ones((tk, LANES), BF16)

    def block(h, q, j, diag_offset):
        hs = slice(h * MLA_PAD, (h + 1) * MLA_PAD)
        start = pl.multiple_of(j * tk, tk)
        s = _dot_nt(q, k_ref[pl.ds(start, tk), hs])
        tiles = [s[:, i * LANES:(i + 1) * LANES] for i in range(ntile)]
        if diag_offset is not None:
            tiles = [jnp.where(lane + (diag_offset + i * LANES) <= row, t, -jnp.inf) for i, t in enumerate(tiles)]
        m_prev = m_ref[h]
        m_tile = tiles[0]
        for t in tiles[1:]:
            m_tile = jnp.maximum(m_tile, t)
        m_new = jnp.maximum(m_prev, jnp.max(m_tile, axis=-1, keepdims=True))
        alpha = jnp.exp(m_prev - m_new)
        p = jnp.concatenate([jnp.exp((t - m_new).astype(BF16)) for t in tiles], axis=1)
        v_ext = jnp.where(own[h], v_ref[pl.ds(start, tk), :], ones)
        acc_ref[h] = alpha * acc_ref[h] + _dot(p, v_ext)
        m_ref[h] = m_new

    ratio = tq // tk
    qs = [q_ref[:, h * MLA_PAD:(h + 1) * MLA_PAD] for h in range(2)]
    for h in range(2):
        m_ref[h] = jnp.full((tq, LANES), -jnp.inf, F32)
        acc_ref[h] = jnp.zeros((tq, LANES), F32)

    def full_blocks(j, count):
        for u in range(count):
            for h in range(2):
                block(h, qs[h], j + u, None)

    n_full = qi * ratio

    def two_blocks(j2, carry):
        full_blocks(2 * j2, 2)
        return carry

    lax.fori_loop(0, n_full // 2, two_blocks, 0)

    @pl.when(n_full % 2 == 1)
    def _():
        full_blocks(n_full - 1, 1)

    for d in range(ratio):
        for h in range(2):
            block(h, qs[h], qi * ratio + d, d * tk)
    outs = [acc_ref[h] / pltpu.roll(acc_ref[h], MLA_V, 1) for h in range(2)]
    o_ref[...] = jnp.where(lane < MLA_V, outs[0], outs[1]).astype(o_ref.dtype)


def _mla(q, k, v, bsz, tlen, tq, tk):
    n = q.shape[0]
    nq = tlen // tq
    hp = MLA_HEADS // 2
    stat = pltpu.VMEM((2, tq, LANES), F32)
    return pl.pallas_call(
        functools.partial(_mla_kernel, tq=tq, tk=tk),
        grid=(bsz, hp, nq),
        in_specs=[pl.BlockSpec((tq, 2 * MLA_PAD), lambda b, h, i: (b * nq + i, h)),
                  pl.BlockSpec((tlen, 2 * MLA_PAD), lambda b, h, i: (b, h)),
                  pl.BlockSpec((tlen, 2 * MLA_V), lambda b, h, i: (b, h))],
        out_specs=pl.BlockSpec((tq, 2 * MLA_V), lambda b, h, i: (b * nq + i, h)),
        out_shape=jax.ShapeDtypeStruct((n, MLA_HEADS * MLA_V), BF16),
        scratch_shapes=[stat, stat],
        compiler_params=pltpu.CompilerParams(
            dimension_semantics=("arbitrary", "arbitrary", "arbitrary"), vmem_limit_bytes=VMEM_LIMIT),
        name="mla_attention",
    )(q, k, v)


def _ffn_kernel(x_ref, ma_ref, mb_ref, woa_ref, wob_ref, gain_ref, wgu_ref, wd_ref, o_ref, h_ref, *, fc):
    d_ff = wd_ref.shape[0]
    x1 = x_ref[...] + _dot(ma_ref[...], woa_ref[...]) + _dot(mb_ref[...], wob_ref[...])
    xn = _rms(x1, gain_ref[...]).astype(BF16)
    for c in range(d_ff // fc):
        gate = _dot(xn, wgu_ref[:, c * fc:(c + 1) * fc])
        up = _dot(xn, wgu_ref[:, d_ff + c * fc:d_ff + (c + 1) * fc])
        h_ref[:, c * fc:(c + 1) * fc] = (_silu(gate) * up).astype(BF16)
    o_ref[...] = x1 + _dot(h_ref[...], wd_ref[...])


def _ffn(x2, mix_a, mix_b, wo_a, wo_b, gain, w_gu, w_down, tm, fc):
    n, d = x2.shape
    d_ff = w_down.shape[0]
    row = lambda i: (i, 0)
    const = lambda i: (0, 0)
    rs = lambda w: pl.BlockSpec((tm, w), row)
    full = lambda a: pl.BlockSpec(a.shape, const, pipeline_mode=pl.Buffered(1))
    return pl.pallas_call(
        functools.partial(_ffn_kernel, fc=fc),
        grid=(n // tm,),
        in_specs=[rs(d), rs(mix_a.shape[1]), rs(mix_b.shape[1]), full(wo_a), full(wo_b), full(gain),
                  full(w_gu), full(w_down)],
        out_specs=rs(d),
        out_shape=jax.ShapeDtypeStruct((n, d), F32),
        scratch_shapes=[pltpu.VMEM((tm, d_ff), BF16)],
        compiler_params=pltpu.CompilerParams(
            dimension_semantics=("arbitrary",), vmem_limit_bytes=VMEM_LIMIT),
        name="outproj_ffn",
    )(x2, mix_a, mix_b, wo_a, wo_b, gain, w_gu, w_down)


def _row(v):
    return v.reshape(1, -1).astype(F32)


def _pad_lanes(v, width, offset=0):
    out = jnp.zeros((1, width), F32)
    return out.at[0, offset:offset + v.shape[-1]].set(v.astype(F32))


def _pad_heads(w, heads, src, dst):
    rows = w.shape[0]
    w = w.reshape(rows, heads, src)
    return jnp.pad(w, ((0, 0), (0, 0), (0, dst - src))).reshape(rows, heads * dst)


def _tile(tlen, want):
    t = min(want, tlen)
    assert tlen % t == 0, (tlen, t)
    return t


def kernel(x, positions, norm_mix, norm_ffn, ffn_w_gu, ffn_w_down, ev_w_in, ev_conv_a, ev_conv_qkv, ev_a_log, ev_dt_bias, ev_out_norm, ev_w_out, od_w_in, od_shift_mu, od_w0, od_w2, od_a0, od_a2, od_g2, od_k_k, od_k_a, od_r_k, od_lnx_w, od_lnx_b, od_vres_w1, od_vres_mu, od_vres_v0, od_vres_v2, od_qa_norm, od_kva_norm, od_w_uq, od_w_ukv, od_q_ln, od_k_ln, od_w_out):
    bsz, tlen, d = x.shape
    n = bsz * tlen
    depth = norm_mix.shape[0]
    tm_proj = _tile(tlen, 256)
    tm_ffn = _tile(tlen, 512)
    tt_gdn = _tile(tlen, 256)
    tt_rwkv = _tile(tlen, 256)
    tq_mla = _tile(tlen, 1024)
    tk_mla = _tile(tq_mla, 512)
    fc = 256

    x2 = x.reshape(n, d)
    tables = _rope_tables(positions, tm_proj)
    v_first = None

    for layer in range(depth):
        gain = _row(norm_mix[layer])
        if layer % 2 == 0:
            e = layer // 2
            n_main = 3 * A_WIDTH + 4 * GDN_WIDTH
            w_t = jnp.swapaxes(ev_w_in[e], 0, 1)
            w_main = _transpose_cast(w_t, n_main)
            w_small = jnp.zeros((d, LANES), F32).at[:, :2 * GDN_HEADS].set(jnp.swapaxes(w_t[n_main:], 0, 1)).astype(BF16)
            alog_l = _pad_lanes(ev_a_log[e], LANES, GDN_HEADS)
            dtb_l = _pad_lanes(ev_dt_bias[e], LANES, GDN_HEADS)
            ya, q, k, v, gate, bg = _ev_in(x2, gain, w_main, w_small, ev_conv_a[e].astype(F32),
                                           ev_conv_qkv[e].astype(F32), alog_l, dtb_l, bsz, tlen, tm_proj)
            o = _gdn(q, k, v, bg, gate, _row(ev_out_norm[e]), bsz, tlen, tt_gdn)
            mix_a, mix_b = ya, o
            w_out = ev_w_out[e].astype(BF16)
            wo_a, wo_b = w_out[:A_WIDTH], w_out[A_WIDTH:]
        else:
            o = layer // 2
            has_vres = o > 0
            w_in = od_w_in[o]
            extra = jnp.zeros((d, LANES), F32).at[:, :MLA_ROPE].set(w_in[:, RWKV_SHIFT_COLS + MLA_Q_LORA + MLA_KV_LORA:])
            mu = jnp.zeros((1, OD_COLS), F32).at[0, :RWKV_SHIFT_COLS].set(od_shift_mu[o])
            if has_vres:
                extra = extra.at[:, MLA_ROPE:MLA_ROPE + RWKV_V_LORA].set(od_vres_w1[o - 1])
                mu = mu.at[0, OD_X + MLA_ROPE:OD_X + MLA_ROPE + RWKV_V_LORA].set(od_vres_mu[o - 1])
            n_mla = MLA_Q_LORA + MLA_KV_LORA
            w_packed = jnp.concatenate(
                [w_in[:, :RWKV_SHIFT_COLS], extra, w_in[:, RWKV_SHIFT_COLS:RWKV_SHIFT_COLS + n_mla]], axis=1).astype(BF16)
            wa2 = jnp.zeros((LANES, 2 * RWKV_WIDTH), F32)
            wa2 = wa2.at[:RWKV_W_LORA, :RWKV_WIDTH].set(od_w2[o]).at[RWKV_W_LORA:, RWKV_WIDTH:].set(od_a2[o]).astype(BF16)
            wuq = _pad_heads(od_w_uq[o], MLA_HEADS, MLA_QK, MLA_PAD).astype(BF16)
            w_ukv = od_w_ukv[o].reshape(MLA_KV_LORA, MLA_HEADS, MLA_NOPE + MLA_V)
            wuk = _pad_heads(w_ukv[:, :, :MLA_NOPE].reshape(MLA_KV_LORA, -1), MLA_HEADS, MLA_NOPE, MLA_PAD).astype(BF16)
            wuv = w_ukv[:, :, MLA_NOPE:].reshape(MLA_KV_LORA, -1).astype(BF16)
            vres = None
            if has_vres:
                v2 = jnp.zeros((LANES, RWKV_WIDTH), F32).at[MLA_ROPE:MLA_ROPE + RWKV_V_LORA].set(od_vres_v2[o - 1])
                vres = (v_first, _row(od_vres_v0[o - 1]), v2.astype(BF16))
            (r, lw, k, v, kk, b, g, q_mla, k_mla, v_mla) = _od_in(
                x2, gain, w_packed, mu, wa2, _row(od_w0[o]), _row(od_a0[o]), od_g2[o].astype(BF16),
                _row(od_k_k[o]), _row(od_k_a[o]), _row(od_qa_norm[o]), _row(od_kva_norm[o]), wuq, wuk, wuv,
                _pad_lanes(od_q_ln[o], MLA_PAD), _pad_lanes(od_k_ln[o], MLA_PAD), tables, vres, bsz, tlen, tm_proj)
            if not has_vres:
                v_first = v
            y_rwkv = _rwkv(r, lw, k, v, kk, b, g, _row(od_r_k[o]), _row(od_lnx_w[o]), _row(od_lnx_b[o]),
                           bsz, tlen, tt_rwkv)
            o_mla = _mla(q_mla, k_mla, v_mla, bsz, tlen, tq_mla, tk_mla)
            mix_a, mix_b = y_rwkv, o_mla
            w_out = od_w_out[o].astype(BF16)
            wo_a, wo_b = w_out[:RWKV_WIDTH], w_out[RWKV_WIDTH:]
        x2 = _ffn(x2, mix_a, mix_b, wo_a, wo_b, _row(norm_ffn[layer]), ffn_w_gu[layer].astype(BF16),
                  ffn_w_down[layer].astype(BF16), tm_ffn, fc)
    return x2.reshape(bsz, tlen, d)
```

```python
import functools

import jax
import jax.numpy as jnp
from jax import lax
from jax.experimental import pallas as pl
from jax.experimental.pallas import tpu as pltpu

F32 = jnp.float32
BF16 = jnp.bfloat16
HIGHEST = lax.Precision.HIGHEST

LANES = 128
SUBLANES = 8
VMEM_LIMIT = 56 * 1024 * 1024

RMS_EPS = 1e-6
L2_EPS = 1e-6
RWKV_LN_EPS = 64e-5
ROPE_THETA = 10000.0

A_WIDTH = 256
GDN_HEADS = 6
GDN_DIM = 128
GDN_WIDTH = GDN_HEADS * GDN_DIM
GDN_CHUNK = 64
GDN_BLOCK = 16

RWKV_HEADS = 8
RWKV_DIM = 64
RWKV_WIDTH = RWKV_HEADS * RWKV_DIM
RWKV_PAIRS = RWKV_WIDTH // LANES
RWKV_CHUNK = 64
RWKV_W_LORA = 64
RWKV_A_LORA = 64
RWKV_V_LORA = 32
RWKV_G_LORA = 128
RWKV_SHIFT_COLS = 3 * RWKV_WIDTH + RWKV_W_LORA + RWKV_A_LORA + RWKV_G_LORA

MLA_HEADS = 8
MLA_NOPE = 64
MLA_ROPE = 32
MLA_V = 64
MLA_QK = MLA_NOPE + MLA_ROPE
MLA_Q_LORA = 512
MLA_KV_LORA = 256
MLA_PAD = 128

HALO = SUBLANES


def _dot(a, b, precision=None):
    return jnp.dot(a, b, preferred_element_type=F32, precision=precision)


def _dot_nt(a, b, precision=None):
    return lax.dot_general(a, b, (((1,), (1,)), ((), ())), preferred_element_type=F32, precision=precision)


def _dot_tn(a, b, precision=None):
    return lax.dot_general(a, b, (((0,), (0,)), ((), ())), preferred_element_type=F32, precision=precision)


def _rms(x, gain):
    return x * lax.rsqrt(jnp.mean(x * x, axis=-1, keepdims=True) + RMS_EPS) * gain


def _sigmoid(x):
    return 1.0 / (1.0 + jnp.exp(-x))


def _silu(x):
    return x * _sigmoid(x)


def _softplus(x):
    return jnp.maximum(x, 0.0) + jnp.log(1.0 + jnp.exp(-jnp.abs(x)))


def _seg64_sum(x):
    lo = lax.broadcasted_iota(jnp.int32, x.shape, 1) < RWKV_DIM
    s_lo = jnp.sum(jnp.where(lo, x, 0.0), axis=-1, keepdims=True)
    s_hi = jnp.sum(jnp.where(lo, 0.0, x), axis=-1, keepdims=True)
    return jnp.where(lo, s_lo, s_hi)


def _shifted_rows(z, scr_ref, carry_ref, first, shifts):
    tm, width = z.shape
    assert width >= 2 * LANES, width
    prev = carry_ref[...]
    scr_ref[0:HALO, :] = jnp.where(first, jnp.zeros_like(prev), prev)
    scr_ref[HALO:HALO + tm, :] = z
    carry_ref[...] = z[tm - HALO:, :]
    return [scr_ref[pl.ds(HALO - s, tm), :] for s in shifts]


def _rope_kernel(pos_ref, invf_ref, c_ref, sa_ref, sb_ref):
    ang = pos_ref[...] * invf_ref[...]
    lane = lax.broadcasted_iota(jnp.int32, ang.shape, 1)
    cos = jnp.cos(ang)
    sin = jnp.sin(ang)
    half = MLA_ROPE // 2
    c_ref[...] = jnp.where(lane < MLA_NOPE, 1.0, jnp.where(lane < MLA_QK, cos, 0.0))
    sa_ref[...] = jnp.where((lane >= MLA_NOPE) & (lane < MLA_NOPE + half), -sin, 0.0)
    sb_ref[...] = jnp.where((lane >= MLA_NOPE + half) & (lane < MLA_QK), sin, 0.0)


def _rope_tables(positions, tm):
    n = positions.size
    pos = jnp.broadcast_to(positions.reshape(n, 1).astype(F32), (n, LANES))
    inv_freq = ROPE_THETA ** (-jnp.arange(0, MLA_ROPE, 2, dtype=F32) / MLA_ROPE)
    half = MLA_ROPE // 2
    invf = jnp.zeros((1, LANES), F32)
    invf = invf.at[0, MLA_NOPE:MLA_NOPE + half].set(inv_freq).at[0, MLA_NOPE + half:MLA_QK].set(inv_freq)
    spec = pl.BlockSpec((tm, LANES), lambda i: (i, 0))
    out = jax.ShapeDtypeStruct((n, LANES), F32)
    return pl.pallas_call(
        _rope_kernel,
        grid=(n // tm,),
        in_specs=[spec, pl.BlockSpec((1, LANES), lambda i: (0, 0))],
        out_specs=[spec, spec, spec],
        out_shape=[out, out, out],
        name="rope_tables",
    )(pos, invf)


def _transpose_cast_kernel(w_ref, o_ref):
    o_ref[...] = w_ref[...].T.astype(o_ref.dtype)


def _transpose_cast(w_t, rows, tn=256):
    d = w_t.shape[1]
    assert rows % tn == 0, (rows, tn)
    return pl.pallas_call(
        _transpose_cast_kernel,
        grid=(rows // tn,),
        in_specs=[pl.BlockSpec((tn, d), lambda i: (i, 0))],
        out_specs=pl.BlockSpec((d, tn), lambda i: (0, i)),
        out_shape=jax.ShapeDtypeStruct((d, rows), BF16),
        name="weight_transpose",
    )(w_t)


def _ev_in_kernel(x_ref, gain_ref, w_ref, ws_ref, ca_ref, cq_ref, alog_ref, dtb_ref,
                  ya_ref, q_ref, k_ref, v_ref, gate_ref, bg_ref,
                  scr_ref, carry_p_ref, carry_qkv_ref):
    first = pl.program_id(1) == 0
    xn = _rms(x_ref[...], gain_ref[...]).astype(BF16)

    za = _dot(xn, w_ref[:, 0:3 * A_WIDTH])
    a_b = za[:, 0:A_WIDTH]
    prod = za[:, A_WIDTH:2 * A_WIDTH] * za[:, 2 * A_WIDTH:3 * A_WIDTH]
    p1, p2 = _shifted_rows(prod, scr_ref.at[:, 0:A_WIDTH], carry_p_ref, first, (1, 2))
    ca = ca_ref[...]
    ya = a_b * (ca[0:1, :] * p2 + ca[1:2, :] * p1 + ca[2:3, :] * prod)
    ya_ref[...] = ya.astype(ya_ref.dtype)

    base = 3 * A_WIDTH
    for g, out_ref in enumerate((q_ref, k_ref, v_ref)):
        cols = slice(base + g * GDN_WIDTH, base + (g + 1) * GDN_WIDTH)
        z = _dot(xn, w_ref[:, cols])
        z1, z2, z3 = _shifted_rows(z, scr_ref, carry_qkv_ref.at[g], first, (1, 2, 3))
        cw = cq_ref[:, g * GDN_WIDTH:(g + 1) * GDN_WIDTH]
        y = _silu(cw[0:1, :] * z3 + cw[1:2, :] * z2 + cw[2:3, :] * z1 + cw[3:4, :] * z)
        for h in range(GDN_HEADS):
            hs = slice(h * GDN_DIM, (h + 1) * GDN_DIM)
            yh = y[:, hs]
            if g < 2:
                yh = yh * lax.rsqrt(jnp.sum(yh * yh, axis=-1, keepdims=True) + L2_EPS)
            if g == 0:
                yh = yh * (GDN_DIM ** -0.5)
            out_ref[:, hs] = yh

    gcols = slice(base + 3 * GDN_WIDTH, base + 4 * GDN_WIDTH)
    gate_ref[...] = _dot(xn, w_ref[:, gcols])

    zs = _dot(xn, ws_ref[...])
    lane = lax.broadcasted_iota(jnp.int32, zs.shape, 1)
    g_log = -jnp.exp(alog_ref[...]) * _softplus(zs + dtb_ref[...])
    bg_ref[...] = jnp.where(lane < GDN_HEADS, _sigmoid(zs), g_log)


def _ev_in(x2, gain, w_main, w_small, conv_a, conv_qkv, alog_l, dtb_l, bsz, tlen, tm):
    n, d = x2.shape
    nt = tlen // tm
    row = lambda b, i: (b * nt + i, 0)
    const = lambda b, i: (0, 0)
    rs = lambda w: pl.BlockSpec((tm, w), row)
    full = lambda a: pl.BlockSpec(a.shape, const)
    outs = [
        jax.ShapeDtypeStruct((n, A_WIDTH), BF16),
        jax.ShapeDtypeStruct((n, GDN_WIDTH), F32),
        jax.ShapeDtypeStruct((n, GDN_WIDTH), F32),
        jax.ShapeDtypeStruct((n, GDN_WIDTH), F32),
        jax.ShapeDtypeStruct((n, GDN_WIDTH), F32),
        jax.ShapeDtypeStruct((n, LANES), F32),
    ]
    return pl.pallas_call(
        _ev_in_kernel,
        grid=(bsz, nt),
        in_specs=[rs(d), full(gain), full(w_main), full(w_small), full(conv_a), full(conv_qkv),
                  full(alog_l), full(dtb_l)],
        out_specs=[rs(A_WIDTH), rs(GDN_WIDTH), rs(GDN_WIDTH), rs(GDN_WIDTH), rs(GDN_WIDTH), rs(LANES)],
        out_shape=outs,
        scratch_shapes=[
            pltpu.VMEM((HALO + tm, GDN_WIDTH), F32),
            pltpu.VMEM((HALO, A_WIDTH), F32),
            pltpu.VMEM((3, HALO, GDN_WIDTH), F32),
        ],
        compiler_params=pltpu.CompilerParams(
            dimension_semantics=("arbitrary", "arbitrary"), vmem_limit_bytes=VMEM_LIMIT),
        name="even_in_proj",
    )(x2, gain, w_main, w_small, conv_a, conv_qkv, alog_l, dtb_l)


def _unit_lower_inverse_many(l_mats, row, col, dot):
    eye = (row == col).astype(F32)
    in_block = (row // GDN_BLOCK) == (col // GDN_BLOCK)
    l_diag = [jnp.where(in_block, m, 0.0) for m in l_mats]
    l_off = [jnp.where(in_block, 0.0, m) for m in l_mats]
    x2 = [dot(m, m) for m in l_diag]
    x4 = [dot(m, m) for m in x2]
    x8 = [dot(m, m) for m in x4]
    d_inv = [eye - m for m in l_diag]
    d_inv = [m + dot(m, x) for m, x in zip(d_inv, x2)]
    d_inv = [m + dot(m, x) for m, x in zip(d_inv, x4)]
    d_inv = [m + dot(m, x) for m, x in zip(d_inv, x8)]
    n1 = [dot(m, x) for m, x in zip(d_inv, l_off)]
    n2 = [dot(m, m) for m in n1]
    outer = [eye - m for m in n1]
    outer = [m + dot(m, x) for m, x in zip(outer, n2)]
    return [dot(m, x) for m, x in zip(outer, d_inv)]


def _gdn_kernel(q_ref, k_ref, v_ref, bg_ref, gate_ref, onorm_ref, o_ref, s_ref, *, nchunk):
    @pl.when(pl.program_id(1) == 0)
    def _():
        s_ref[...] = jnp.zeros_like(s_ref)

    c = GDN_CHUNK
    dk = GDN_DIM
    pairs = range(GDN_HEADS // 2)
    row_c = lax.broadcasted_iota(jnp.int32, (c, c), 0)
    col_c = lax.broadcasted_iota(jnp.int32, (c, c), 1)
    tri = (row_c >= col_c).astype(F32)
    row1 = lax.broadcasted_iota(jnp.int32, (2 * c, 2 * c), 0)
    col1 = lax.broadcasted_iota(jnp.int32, (2 * c, 2 * c), 1)
    same_head = (row1 // c) == (col1 // c)
    lag = (row1 % c) - (col1 % c)
    causal = same_head & (lag >= 0)
    strict = same_head & (lag > 0)
    tri_dup = (lax.broadcasted_iota(jnp.int32, (c, 2 * c), 0) <= lax.broadcasted_iota(jnp.int32, (c, 2 * c), 1) % c).astype(F32)
    first_half = lax.broadcasted_iota(jnp.int32, (1, 2 * c), 1) < c
    head0_rows = (lax.broadcasted_iota(jnp.int32, (2 * c, 1), 0) < c).astype(F32)
    onorm = onorm_ref[...]

    def bdot(a, b):
        return _dot(a.astype(BF16), b.astype(BF16))

    def rows2(a, b):
        return jnp.concatenate([a, b], axis=0)

    items = [(ci, p) for ci in range(nchunk) for p in pairs]
    q_s, k_s, v_s, beta_s, gc_s, gl_s, l_mats, qk = [], [], [], [], [], [], [], []
    for ci in range(nchunk):
        rows = slice(ci * c, (ci + 1) * c)
        bg = bg_ref[rows, :]
        g_cols = _dot(tri, bg, precision=HIGHEST)
        g_rows = _dot_tn(bg, tri_dup, precision=HIGHEST)
        for p in pairs:
            h0, h1 = 2 * p, 2 * p + 1
            s0, s1 = slice(h0 * dk, (h0 + 1) * dk), slice(h1 * dk, (h1 + 1) * dk)
            q_s.append(rows2(q_ref[rows, s0], q_ref[rows, s1]))
            k_s.append(rows2(k_ref[rows, s0], k_ref[rows, s1]))
            v_s.append(rows2(v_ref[rows, s0], v_ref[rows, s1]))
            beta_s.append(rows2(bg[:, h0:h0 + 1], bg[:, h1:h1 + 1]))
            gc0 = g_cols[:, GDN_HEADS + h0:GDN_HEADS + h0 + 1]
            gc1 = g_cols[:, GDN_HEADS + h1:GDN_HEADS + h1 + 1]
            gc_s.append(rows2(gc0, gc1))
            gl_s.append(rows2(jnp.broadcast_to(gc0[c - 1:c, :], (c, 1)), jnp.broadcast_to(gc1[c - 1:c, :], (c, 1))))
            gr = jnp.where(first_half, g_rows[GDN_HEADS + h0:GDN_HEADS + h0 + 1, :], g_rows[GDN_HEADS + h1:GDN_HEADS + h1 + 1, :])
            diff = gc_s[-1] - gr
            decay = jnp.where(causal, jnp.exp(jnp.where(causal, diff, 0.0)), 0.0)
            kb = k_s[-1] * beta_s[-1]
            score = _dot_nt(rows2(kb, q_s[-1]).astype(BF16), k_s[-1].astype(BF16))
            l_mats.append(jnp.where(strict, score[0:2 * c] * decay, 0.0))
            qk.append(score[2 * c:4 * c] * decay)
    t_inv = _unit_lower_inverse_many(l_mats, row1, col1, bdot)
    eg = [jnp.exp(g) for g in gc_s]
    wu = [bdot(t_inv[i], jnp.concatenate([k_s[i] * (beta_s[i] * eg[i]), v_s[i] * beta_s[i]], axis=1)) for i in range(len(items))]
    qy = [bdot(qk[i], wu[i]) for i in range(len(items))]
    gd = []
    for i in range(len(items)):
        k_dec = k_s[i] * jnp.exp(gl_s[i] - gc_s[i])
        wu0 = wu[i] * head0_rows
        wu1 = wu[i] - wu0
        expanded = jnp.concatenate([wu0[:, 0:dk], wu1[:, 0:dk], wu0[:, dk:], wu1[:, dk:]], axis=1)
        gd.append(_dot_tn(k_dec.astype(BF16), expanded.astype(BF16)))

    for i, (ci, p) in enumerate(items):
        rows = slice(ci * c, (ci + 1) * c)
        q_hat = q_s[i] * eg[i] - qy[i][:, 0:dk]
        y0 = qy[i][:, dk:]
        for hh in range(2):
            h = 2 * p + hh
            hs = slice(h * dk, (h + 1) * dk)
            hr = slice(hh * c, (hh + 1) * c)
            state = s_ref[h]
            o = bdot(q_hat[hr], state) + y0[hr]
            g_last = gl_s[i][hh * c:hh * c + 1, :]
            s_ref[h] = state * jnp.exp(g_last) - bdot(gd[i][:, hh * dk:(hh + 1) * dk], state) + gd[i][:, (2 + hh) * dk:(3 + hh) * dk]
            o = _rms(o, onorm) * _silu(gate_ref[rows, hs])
            o_ref[rows, hs] = o.astype(o_ref.dtype)


def _gdn(q, k, v, bg, gate, onorm, bsz, tlen, tt):
    n = q.shape[0]
    nt = tlen // tt
    nchunk = tt // GDN_CHUNK
    row = lambda b, i: (b * nt + i, 0)
    rs = lambda w: pl.BlockSpec((tt, w), row)
    return pl.pallas_call(
        functools.partial(_gdn_kernel, nchunk=nchunk),
        grid=(bsz, nt),
        in_specs=[rs(GDN_WIDTH), rs(GDN_WIDTH), rs(GDN_WIDTH), rs(LANES),
                  rs(GDN_WIDTH), pl.BlockSpec(onorm.shape, lambda b, i: (0, 0))],
        out_specs=rs(GDN_WIDTH),
        out_shape=jax.ShapeDtypeStruct((n, GDN_WIDTH), BF16),
        scratch_shapes=[pltpu.VMEM((GDN_HEADS, GDN_DIM, GDN_DIM), F32)],
        compiler_params=pltpu.CompilerParams(
            dimension_semantics=("arbitrary", "arbitrary"), vmem_limit_bytes=VMEM_LIMIT),
        name="gated_delta_rule",
    )(q, k, v, bg, gate, onorm)


OD_R = 0
OD_K = RWKV_WIDTH
OD_V = 2 * RWKV_WIDTH
OD_WA = 3 * RWKV_WIDTH
OD_G = OD_WA + LANES
OD_X = RWKV_SHIFT_COLS
OD_SMALL = 3 * LANES
OD_CQ = OD_X + LANES
OD_CKV = OD_CQ + MLA_Q_LORA
OD_COLS = OD_CKV + MLA_KV_LORA


def _od_in_kernel(*refs, has_vres):
    (x_ref, gain_ref, w_ref, mu_ref, wa2_ref, w0_ref, a0_ref, g2_ref, kk_ref, ka_ref,
     qan_ref, kvan_ref, wuq_ref, wuk_ref, wuv_ref, qln_ref, kln_ref, rot_ref, seg_ref,
     c_ref, sa_ref, sb_ref) = refs[:22]
    pos = 22
    if has_vres:
        vfirst_ref, v0_ref, v2_ref = refs[pos:pos + 3]
        pos += 3
    (r_out, lw_out, k_out, v_out, kkn_out, b_out, g_out, q_out, kmla_out, vmla_out) = refs[pos:pos + 10]
    scr_ref, carry_ref = refs[pos + 10:]

    first = pl.program_id(1) == 0
    xn = _rms(x_ref[...], gain_ref[...]).astype(BF16)

    def lerped(start, width, slot):
        z = _dot(xn, w_ref[:, start:start + width])
        (zp,) = _shifted_rows(z, scr_ref.at[:, 0:width], carry_ref.at[:, slot:slot + width], first, (1,))
        return z, z + mu_ref[:, start:start + width] * (zp - z)

    small, small_l = lerped(OD_WA, OD_SMALL, OD_WA)
    wa = small_l[:, 0:LANES]
    gd = small_l[:, LANES:2 * LANES]
    zx = small[:, 2 * LANES:3 * LANES]
    zx_l = small_l[:, 2 * LANES:3 * LANES]

    two = lambda t: jnp.concatenate([t, t], axis=1)
    cos2 = two(c_ref[...])
    sin2 = two(sb_ref[...] - sa_ref[...])

    cq = _dot(xn, w_ref[:, OD_CQ:OD_CQ + MLA_Q_LORA])
    q_all = _dot(_rms(cq, qan_ref[...]).astype(BF16), wuq_ref[...])
    ckv = _dot(xn, w_ref[:, OD_CKV:OD_CKV + MLA_KV_LORA])
    ckv_n = _rms(ckv, kvan_ref[...]).astype(BF16)
    k_all = _dot(ckv_n, wuk_ref[...])
    vmla_out[...] = _dot(ckv_n, wuv_ref[...]).astype(vmla_out.dtype)
    lane = lax.broadcasted_iota(jnp.int32, zx.shape, 1)
    k_rope2 = two(jnp.where((lane >= MLA_NOPE) & (lane < MLA_QK), pltpu.roll(zx, MLA_NOPE, 1), 0.0))
    q_ln2 = two(qln_ref[...])
    k_ln2 = two(kln_ref[...])
    scale = MLA_QK ** -0.5

    def norm_rope(t, gain2):
        ms = _dot((t * t).astype(BF16), seg_ref[...]) * (1.0 / MLA_QK)
        t = t * lax.rsqrt(ms + RMS_EPS) * gain2
        return t * cos2 + _dot(t.astype(BF16), rot_ref[...]) * sin2

    def mla_heads(pairs):
        for p in pairs:
            hs = slice(2 * p * MLA_PAD, (2 * p + 2) * MLA_PAD)
            q_out[:, hs] = (norm_rope(q_all[:, hs], q_ln2) * scale).astype(q_out.dtype)
            kmla_out[:, hs] = norm_rope(k_all[:, hs] + k_rope2, k_ln2).astype(kmla_out.dtype)

    _, r = lerped(OD_R, RWKV_WIDTH, OD_R)
    r_out[...] = r
    mla_heads((0,))

    lora_in = jnp.where(lane < RWKV_W_LORA, jnp.tanh(wa), wa).astype(BF16)
    lora = _dot(lora_in, wa2_ref[...])
    w_log = -_softplus(-(w0_ref[...] + lora[:, 0:RWKV_WIDTH])) - 0.5
    lw_out[...] = -jnp.exp(w_log)
    a = _sigmoid(a0_ref[...] + lora[:, RWKV_WIDTH:2 * RWKV_WIDTH])
    mla_heads((1, 2))

    _, k = lerped(OD_K, RWKV_WIDTH, OD_K)
    kx = k * kk_ref[...]
    for p in range(RWKV_PAIRS):
        ps = slice(p * LANES, (p + 1) * LANES)
        kxp = kx[:, ps]
        kkp = kxp * lax.rsqrt(_seg64_sum(kxp * kxp) + L2_EPS)
        kkn_out[:, ps] = kkp
        b_out[:, ps] = kkp * a[:, ps]
    k_out[...] = k * (1.0 + (a - 1.0) * ka_ref[...])
    mla_heads((3,))

    _, v = lerped(OD_V, RWKV_WIDTH, OD_V)
    if has_vres:
        mix = _sigmoid(v0_ref[...] + _dot(zx_l.astype(BF16), v2_ref[...]))
        v = v + (vfirst_ref[...] - v) * mix
    v_out[...] = v
    g_out[...] = _dot(_sigmoid(gd).astype(BF16), g2_ref[...])


def _od_in(x2, gain, w_in, mu, wa2, w0, a0, g2, k_k, k_a, qan, kvan, wuq, wuk, wuv, qln, kln, rot2, seg2,
           tables, vres, bsz, tlen, tm):
    n, d = x2.shape
    nt = tlen // tm
    row = lambda b, i: (b * nt + i, 0)
    const = lambda b, i: (0, 0)
    rs = lambda w: pl.BlockSpec((tm, w), row)
    full = lambda a: pl.BlockSpec(a.shape, const)
    params = [gain, w_in, mu, wa2, w0, a0, g2, k_k, k_a, qan, kvan, wuq, wuk, wuv, qln, kln, rot2, seg2]
    args = [x2] + params + list(tables)
    in_specs = [rs(d)] + [full(a) for a in params] + [rs(LANES)] * 3
    if vres is not None:
        v_first, v0, v2 = vres
        args += [v_first, v0, v2]
        in_specs += [rs(RWKV_WIDTH), full(v0), full(v2)]
    f32w = lambda w: jax.ShapeDtypeStruct((n, w), F32)
    bfw = lambda w: jax.ShapeDtypeStruct((n, w), BF16)
    outs = [f32w(RWKV_WIDTH)] * 7 + [bfw(MLA_HEADS * MLA_PAD), bfw(MLA_HEADS * MLA_PAD), bfw(MLA_HEADS * MLA_V)]
    out_specs = [rs(RWKV_WIDTH)] * 7 + [rs(MLA_HEADS * MLA_PAD), rs(MLA_HEADS * MLA_PAD), rs(MLA_HEADS * MLA_V)]
    carry_w = RWKV_SHIFT_COLS + LANES
    return pl.pallas_call(
        functools.partial(_od_in_kernel, has_vres=vres is not None),
        grid=(bsz, nt),
        in_specs=in_specs,
        out_specs=out_specs,
        out_shape=outs,
        scratch_shapes=[pltpu.VMEM((HALO + tm, RWKV_WIDTH), F32), pltpu.VMEM((HALO, carry_w), F32)],
        compiler_params=pltpu.CompilerParams(
            dimension_semantics=("arbitrary", "arbitrary"), vmem_limit_bytes=VMEM_LIMIT),
        name="odd_in_proj",
    )(*args)


def _rwkv_kernel(r_ref, lw_ref, k_ref, v_ref, kk_ref, b_ref, g_ref, rk_ref, lnw_ref, lnb_ref,
                 o_ref, h_ref, y_ref, *, nchunk):
    @pl.when(pl.program_id(1) == 0)
    def _():
        h_ref[...] = jnp.zeros_like(h_ref)

    c = RWKV_CHUNK
    d = RWKV_DIM
    lane = lax.broadcasted_iota(jnp.int32, (1, LANES), 1)
    m_lo = (lane < d).astype(F32)
    m_hi = 1.0 - m_lo
    row_c = lax.broadcasted_iota(jnp.int32, (c, c), 0)
    col_c = lax.broadcasted_iota(jnp.int32, (c, c), 1)
    tri = (row_c >= col_c).astype(F32)
    row1 = lax.broadcasted_iota(jnp.int32, (2 * c, 2 * c), 0)
    col1 = lax.broadcasted_iota(jnp.int32, (2 * c, 2 * c), 1)
    eye1 = row1 == col1
    row2 = lax.broadcasted_iota(jnp.int32, (4 * c, 4 * c), 0)
    col2 = lax.broadcasted_iota(jnp.int32, (4 * c, 4 * c), 1)
    same_head = ((row2 // c) % 2) == ((col2 // c) % 2)
    min_lag = jnp.where(row2 < 2 * c, 1, 0)
    keep = same_head & ((row2 % c) - (col2 % c) >= min_lag)

    def bdot(a, b):
        return _dot(a.astype(BF16), b.astype(BF16))

    def stack(x):
        return jnp.concatenate([x * m_lo, x * m_hi], axis=0)

    pairs = range(RWKV_PAIRS)
    ps = [slice(p * LANES, (p + 1) * LANES) for p in pairs]
    zero = jnp.zeros((2 * c, LANES), F32)
    items = [(ci, p) for ci in range(nchunk) for p in pairs]
    n_items = range(len(items))
    qt_p, w_last_p, am_s, v_s, lhs1, rhs1, lhs_t = [], [], [], [], [], [], []
    for ci in range(nchunk):
        rows = slice(ci * c, (ci + 1) * c)
        lw = lw_ref[rows, :]
        logw = _dot(tri, lw, precision=HIGHEST)
        logw_last = logw[c - 1:c, :]
        w_inv = jnp.exp(-logw)
        w_last = jnp.exp(logw_last)
        to_end = jnp.exp(logw_last - logw)
        am = kk_ref[rows, :] * jnp.exp(logw - lw)
        qt = r_ref[rows, :] * jnp.exp(logw)
        bp = b_ref[rows, :] * w_inv
        kp = k_ref[rows, :] * w_inv
        bh = b_ref[rows, :] * to_end
        kh = k_ref[rows, :] * to_end
        v = v_ref[rows, :]
        for s in ps:
            qt_p.append(qt[:, s])
            w_last_p.append(w_last[:, s])
            am_s.append(stack(am[:, s]))
            v_s.append(stack(v[:, s]))
            lhs1.append(jnp.concatenate([am_s[-1], stack(qt[:, s])], axis=0).astype(BF16))
            rhs1.append(jnp.concatenate([bp[:, s], bp[:, s], kp[:, s], kp[:, s]], axis=0).astype(BF16))
            lhs_t.append(jnp.concatenate([stack(bh[:, s]), stack(kh[:, s])], axis=0).astype(BF16))
    score = [jnp.where(keep, _dot_nt(lhs1[i], rhs1[i]), 0.0) for i in n_items]
    t_inv = _unit_lower_inverse_many([sc[0:2 * c, 0:2 * c] for sc in score], row1, col1, bdot)
    r1 = [bdot(score[i][0:2 * c, 2 * c:4 * c], v_s[i]) for i in n_items]
    x = [bdot(t_inv[i], jnp.concatenate([am_s[i], r1[i]], axis=1)) for i in n_items]
    rhs2 = [jnp.concatenate([jnp.concatenate([x[i][:, 0:LANES], -x[i][:, LANES:]], axis=1),
                             jnp.concatenate([zero, v_s[i]], axis=1)], axis=0).astype(BF16) for i in n_items]
    qy = [_dot(score[i][2 * c:4 * c, :].astype(BF16), rhs2[i]) for i in n_items]
    gd = [_dot_tn(lhs_t[i], rhs2[i]) for i in n_items]

    for i, (ci, p) in enumerate(items):
        rows = slice(ci * c, (ci + 1) * c)
        q_hat = qt_p[i] - (qy[i][0:c, 0:LANES] + qy[i][c:2 * c, 0:LANES])
        y0 = qy[i][0:c, LANES:] + qy[i][c:2 * c, LANES:]
        h = h_ref[p]
        y_ref[rows, ps[p]] = bdot(q_hat, h) + y0
        wl_col = jnp.sum(jnp.where(eye1, w_last_p[i], 0.0), axis=1, keepdims=True)
        h_ref[p] = wl_col * h - bdot(gd[i][:, 0:LANES], h) + gd[i][:, LANES:]

    for p in pairs:
        s = slice(p * LANES, (p + 1) * LANES)
        y = y_ref[:, s]
        mu = _seg64_sum(y) * (1.0 / d)
        yc = y - mu
        var = _seg64_sum(yc * yc) * (1.0 / d)
        y = yc * lax.rsqrt(var + RWKV_LN_EPS) * lnw_ref[:, s] + lnb_ref[:, s]
        bonus = _seg64_sum(r_ref[:, s] * k_ref[:, s] * rk_ref[:, s])
        y = (y + bonus * v_ref[:, s]) * g_ref[:, s]
        o_ref[:, s] = y.astype(o_ref.dtype)


def _rwkv(r, lw, k, v, kk, b, g, rk, lnw, lnb, bsz, tlen, tt):
    n, width = r.shape
    nt = tlen // tt
    blk = pl.BlockSpec((tt, width), lambda bb, i: (bb * nt + i, 0))
    par = pl.BlockSpec((1, width), lambda bb, i: (0, 0))
    return pl.pallas_call(
        functools.partial(_rwkv_kernel, nchunk=tt // RWKV_CHUNK),
        grid=(bsz, nt),
        in_specs=[blk] * 7 + [par] * 3,
        out_specs=blk,
        out_shape=jax.ShapeDtypeStruct((n, width), BF16),
        scratch_shapes=[pltpu.VMEM((RWKV_PAIRS, LANES, LANES), F32), pltpu.VMEM((tt, width), F32)],
        compiler_params=pltpu.CompilerParams(
            dimension_semantics=("arbitrary", "arbitrary"), vmem_limit_bytes=VMEM_LIMIT),
        name="rwkv7_chunked",
    )(r, lw, k, v, kk, b, g, rk, lnw, lnb)


def _mla_kernel(q_ref, k_ref, v_ref, o_ref, m_ref, acc_ref, *, tq, tk, nh):
    qi = pl.program_id(2)
    ntile = tk // LANES
    row = lax.broadcasted_iota(jnp.int32, (tq, LANES), 0)
    lane = lax.broadcasted_iota(jnp.int32, (tq, LANES), 1)
    own = [lax.broadcasted_iota(jnp.int32, (tk, LANES), 1) < MLA_V,
           lax.broadcasted_iota(jnp.int32, (tk, LANES), 1) >= MLA_V]
    ones = jnp.ones((tk, LANES), BF16)

    def block(h, q, j, masked, r0=0):
        hs = slice(h * MLA_PAD, (h + 1) * MLA_PAD)
        nrow = tq - r0
        start = pl.multiple_of(j * tk, tk)
        s = _dot_nt(q, k_ref[pl.ds(start, tk), hs])
        tiles = [s[:, i * LANES:(i + 1) * LANES] for i in range(ntile)]
        if masked:
            row_l = lax.broadcasted_iota(jnp.int32, (nrow, LANES), 0)
            lane_l = lax.broadcasted_iota(jnp.int32, (nrow, LANES), 1)
            tiles = [jnp.where(lane_l + i * LANES <= row_l, t, -jnp.inf) for i, t in enumerate(tiles)]
        m_prev = m_ref[h, r0:, :]
        m_tile = tiles[0]
        for t in tiles[1:]:
            m_tile = jnp.maximum(m_tile, t)
        m_new = jnp.maximum(m_prev, jnp.max(m_tile, axis=-1, keepdims=True))
        alpha = jnp.exp(m_prev - m_new)
        p = jnp.concatenate([jnp.exp((t - m_new).astype(BF16)) for t in tiles], axis=1)
        v_ext = jnp.where(own[h % 2], v_ref[pl.ds(start, tk), (h // 2) * LANES:(h // 2 + 1) * LANES], ones)
        acc_ref[h, r0:, :] = alpha * acc_ref[h, r0:, :] + _dot(p, v_ext)
        m_ref[h, r0:, :] = m_new

    ratio = tq // tk
    qs = [q_ref[:, h * MLA_PAD:(h + 1) * MLA_PAD] for h in range(nh)]
    for h in range(nh):
        m_ref[h] = jnp.full((tq, LANES), -jnp.inf, F32)
        acc_ref[h] = jnp.zeros((tq, LANES), F32)

    def full_blocks(j, count):
        for u in range(count):
            for h in range(nh):
                block(h, qs[h], j + u, False)

    n_full = qi * ratio

    def two_blocks(j2, carry):
        full_blocks(2 * j2, 2)
        return carry

    lax.fori_loop(0, n_full // 2, two_blocks, 0)

    @pl.when(n_full % 2 == 1)
    def _():
        full_blocks(n_full - 1, 1)

    for d in range(ratio):
        for h in range(nh):
            block(h, q_ref[d * tk:, h * MLA_PAD:(h + 1) * MLA_PAD], qi * ratio + d, True, d * tk)
    outs = [acc_ref[h] / pltpu.roll(acc_ref[h], MLA_V, 1) for h in range(nh)]
    for p in range(nh // 2):
        o_ref[:, p * LANES:(p + 1) * LANES] = jnp.where(lane < MLA_V, outs[2 * p], outs[2 * p + 1]).astype(o_ref.dtype)


def _mla(q, k, v, bsz, tlen, tq, tk, nh=4):
    n = q.shape[0]
    nq = tlen // tq
    hp = MLA_HEADS // nh
    stat = pltpu.VMEM((nh, tq, LANES), F32)
    return pl.pallas_call(
        functools.partial(_mla_kernel, tq=tq, tk=tk, nh=nh),
        grid=(bsz, hp, nq),
        in_specs=[pl.BlockSpec((tq, nh * MLA_PAD), lambda b, h, i: (b * nq + i, h)),
                  pl.BlockSpec((tlen, nh * MLA_PAD), lambda b, h, i: (b, h)),
                  pl.BlockSpec((tlen, nh * MLA_V), lambda b, h, i: (b, h))],
        out_specs=pl.BlockSpec((tq, nh * MLA_V), lambda b, h, i: (b * nq + i, h)),
        out_shape=jax.ShapeDtypeStruct((n, MLA_HEADS * MLA_V), BF16),
        scratch_shapes=[stat, stat],
        compiler_params=pltpu.CompilerParams(
            dimension_semantics=("arbitrary", "arbitrary", "arbitrary"), vmem_limit_bytes=VMEM_LIMIT),
        name="mla_attention",
    )(q, k, v)


def _ffn_kernel(x_ref, ma_ref, mb_ref, woa_ref, wob_ref, gain_ref, wgu_ref, wd_ref, o_ref, h_ref, *, fc):
    d_ff = wd_ref.shape[0]
    x1 = x_ref[...] + _dot(ma_ref[...], woa_ref[...]) + _dot(mb_ref[...], wob_ref[...])
    xn = _rms(x1, gain_ref[...]).astype(BF16)
    for c in range(d_ff // fc):
        gate = _dot(xn, wgu_ref[:, c * fc:(c + 1) * fc])
        up = _dot(xn, wgu_ref[:, d_ff + c * fc:d_ff + (c + 1) * fc])
        h_ref[:, c * fc:(c + 1) * fc] = (_silu(gate) * up).astype(BF16)
    o_ref[...] = x1 + _dot(h_ref[...], wd_ref[...])


def _ffn(x2, mix_a, mix_b, wo_a, wo_b, gain, w_gu, w_down, tm, fc):
    n, d = x2.shape
    d_ff = w_down.shape[0]
    row = lambda i: (i, 0)
    const = lambda i: (0, 0)
    rs = lambda w: pl.BlockSpec((tm, w), row)
    full = lambda a: pl.BlockSpec(a.shape, const, pipeline_mode=pl.Buffered(1))
    return pl.pallas_call(
        functools.partial(_ffn_kernel, fc=fc),
        grid=(n // tm,),
        in_specs=[rs(d), rs(mix_a.shape[1]), rs(mix_b.shape[1]), full(wo_a), full(wo_b), full(gain),
                  full(w_gu), full(w_down)],
        out_specs=rs(d),
        out_shape=jax.ShapeDtypeStruct((n, d), F32),
        scratch_shapes=[pltpu.VMEM((tm, d_ff), BF16)],
        compiler_params=pltpu.CompilerParams(
            dimension_semantics=("arbitrary",), vmem_limit_bytes=VMEM_LIMIT),
        name="outproj_ffn",
    )(x2, mix_a, mix_b, wo_a, wo_b, gain, w_gu, w_down)


def _row(v):
    return v.reshape(1, -1).astype(F32)


def _pad_lanes(v, width, offset=0):
    out = jnp.zeros((1, width), F32)
    return out.at[0, offset:offset + v.shape[-1]].set(v.astype(F32))


def _pad_heads(w, heads, src, dst):
    rows = w.shape[0]
    w = w.reshape(rows, heads, src)
    return jnp.pad(w, ((0, 0), (0, 0), (0, dst - src))).reshape(rows, heads * dst)


def _tile(tlen, want):
    t = min(want, tlen)
    assert tlen % t == 0, (tlen, t)
    return t


def kernel(x, positions, norm_mix, norm_ffn, ffn_w_gu, ffn_w_down, ev_w_in, ev_conv_a, ev_conv_qkv, ev_a_log, ev_dt_bias, ev_out_norm, ev_w_out, od_w_in, od_shift_mu, od_w0, od_w2, od_a0, od_a2, od_g2, od_k_k, od_k_a, od_r_k, od_lnx_w, od_lnx_b, od_vres_w1, od_vres_mu, od_vres_v0, od_vres_v2, od_qa_norm, od_kva_norm, od_w_uq, od_w_ukv, od_q_ln, od_k_ln, od_w_out):
    bsz, tlen, d = x.shape
    n = bsz * tlen
    depth = norm_mix.shape[0]
    tm_proj = _tile(tlen, 256)
    tm_ffn = _tile(tlen, 512)
    tt_gdn = _tile(tlen, 256)
    tt_rwkv = _tile(tlen, 256)
    tq_mla = _tile(tlen, 1024)
    tk_mla = _tile(tq_mla, 512)
    fc = 256

    x2 = x.reshape(n, d)
    tables = _rope_tables(positions, tm_proj)
    half = MLA_ROPE // 2
    idx = jnp.arange(MLA_NOPE, MLA_NOPE + half)
    rot = jnp.zeros((MLA_PAD, MLA_PAD), F32).at[idx + half, idx].set(-1.0).at[idx, idx + half].set(1.0)
    eye2 = jnp.eye(2, dtype=F32)
    rot2 = jnp.kron(eye2, rot).astype(BF16)
    seg2 = jnp.kron(eye2, jnp.ones((MLA_PAD, MLA_PAD), F32)).astype(BF16)
    v_first = None

    for layer in range(depth):
        gain = _row(norm_mix[layer])
        if layer % 2 == 0:
            e = layer // 2
            n_main = 3 * A_WIDTH + 4 * GDN_WIDTH
            w_t = jnp.swapaxes(ev_w_in[e], 0, 1)
            w_main = _transpose_cast(w_t, n_main)
            w_small = jnp.zeros((d, LANES), F32).at[:, :2 * GDN_HEADS].set(jnp.swapaxes(w_t[n_main:], 0, 1)).astype(BF16)
            alog_l = _pad_lanes(ev_a_log[e], LANES, GDN_HEADS)
            dtb_l = _pad_lanes(ev_dt_bias[e], LANES, GDN_HEADS)
            ya, q, k, v, gate, bg = _ev_in(x2, gain, w_main, w_small, ev_conv_a[e].astype(F32),
                                           ev_conv_qkv[e].astype(F32), alog_l, dtb_l, bsz, tlen, tm_proj)
            o = _gdn(q, k, v, bg, gate, _row(ev_out_norm[e]), bsz, tlen, tt_gdn)
            mix_a, mix_b = ya, o
            w_out = ev_w_out[e].astype(BF16)
            wo_a, wo_b = w_out[:A_WIDTH], w_out[A_WIDTH:]
        else:
            o = layer // 2
            has_vres = o > 0
            w_in = od_w_in[o]
            extra = jnp.zeros((d, LANES), F32).at[:, :MLA_ROPE].set(w_in[:, RWKV_SHIFT_COLS + MLA_Q_LORA + MLA_KV_LORA:])
            mu = jnp.zeros((1, OD_COLS), F32).at[0, :RWKV_SHIFT_COLS].set(od_shift_mu[o])
            if has_vres:
                extra = extra.at[:, MLA_ROPE:MLA_ROPE + RWKV_V_LORA].set(od_vres_w1[o - 1])
                mu = mu.at[0, OD_X + MLA_ROPE:OD_X + MLA_ROPE + RWKV_V_LORA].set(od_vres_mu[o - 1])
            n_mla = MLA_Q_LORA + MLA_KV_LORA
            w_packed = jnp.concatenate(
                [w_in[:, :RWKV_SHIFT_COLS], extra, w_in[:, RWKV_SHIFT_COLS:RWKV_SHIFT_COLS + n_mla]], axis=1).astype(BF16)
            wa2 = jnp.zeros((LANES, 2 * RWKV_WIDTH), F32)
            wa2 = wa2.at[:RWKV_W_LORA, :RWKV_WIDTH].set(od_w2[o]).at[RWKV_W_LORA:, RWKV_WIDTH:].set(od_a2[o]).astype(BF16)
            wuq = _pad_heads(od_w_uq[o], MLA_HEADS, MLA_QK, MLA_PAD).astype(BF16)
            w_ukv = od_w_ukv[o].reshape(MLA_KV_LORA, MLA_HEADS, MLA_NOPE + MLA_V)
            wuk = _pad_heads(w_ukv[:, :, :MLA_NOPE].reshape(MLA_KV_LORA, -1), MLA_HEADS, MLA_NOPE, MLA_PAD).astype(BF16)
            wuv = w_ukv[:, :, MLA_NOPE:].reshape(MLA_KV_LORA, -1).astype(BF16)
            vres = None
            if has_vres:
                v2 = jnp.zeros((LANES, RWKV_WIDTH), F32).at[MLA_ROPE:MLA_ROPE + RWKV_V_LORA].set(od_vres_v2[o - 1])
                vres = (v_first, _row(od_vres_v0[o - 1]), v2.astype(BF16))
            (r, lw, k, v, kk, b, g, q_mla, k_mla, v_mla) = _od_in(
                x2, gain, w_packed, mu, wa2, _row(od_w0[o]), _row(od_a0[o]), od_g2[o].astype(BF16),
                _row(od_k_k[o]), _row(od_k_a[o]), _row(od_qa_norm[o]), _row(od_kva_norm[o]), wuq, wuk, wuv,
                _pad_lanes(od_q_ln[o], MLA_PAD), _pad_lanes(od_k_ln[o], MLA_PAD), rot2, seg2, tables, vres, bsz, tlen, tm_proj)
            if not has_vres:
                v_first = v
            y_rwkv = _rwkv(r, lw, k, v, kk, b, g, _row(od_r_k[o]), _row(od_lnx_w[o]), _row(od_lnx_b[o]),
                           bsz, tlen, tt_rwkv)
            o_mla = _mla(q_mla, k_mla, v_mla, bsz, tlen, tq_mla, tk_mla)
            mix_a, mix_b = y_rwkv, o_mla
            w_out = od_w_out[o].astype(BF16)
            wo_a, wo_b = w_out[:RWKV_WIDTH], w_out[RWKV_WIDTH:]
        x2 = _ffn(x2, mix_a, mix_b, wo_a, wo_b, _row(norm_ffn[layer]), ffn_w_gu[layer].astype(BF16),
                  ffn_w_down[layer].astype(BF16), tm_ffn, fc)
    return x2.reshape(bsz, tlen, d)
```

```python
import functools

import jax
import jax.numpy as jnp
from jax import lax
from jax.experimental import pallas as pl
from jax.experimental.pallas import tpu as pltpu

F32 = jnp.float32
BF16 = jnp.bfloat16
HIGHEST = lax.Precision.HIGHEST

LANES = 128
SUBLANES = 8
VMEM_LIMIT = 56 * 1024 * 1024

RMS_EPS = 1e-6
L2_EPS = 1e-6
RWKV_LN_EPS = 64e-5
ROPE_THETA = 10000.0

A_WIDTH = 256
GDN_HEADS = 6
GDN_DIM = 128
GDN_WIDTH = GDN_HEADS * GDN_DIM
GDN_CHUNK = 64
GDN_BLOCK = 16

RWKV_HEADS = 8
RWKV_DIM = 64
RWKV_WIDTH = RWKV_HEADS * RWKV_DIM
RWKV_PAIRS = RWKV_WIDTH // LANES
RWKV_CHUNK = 64
RWKV_W_LORA = 64
RWKV_A_LORA = 64
RWKV_V_LORA = 32
RWKV_G_LORA = 128
RWKV_SHIFT_COLS = 3 * RWKV_WIDTH + RWKV_W_LORA + RWKV_A_LORA + RWKV_G_LORA

MLA_HEADS = 8
MLA_NOPE = 64
MLA_ROPE = 32
MLA_V = 64
MLA_QK = MLA_NOPE + MLA_ROPE
MLA_Q_LORA = 512
MLA_KV_LORA = 256
MLA_PAD = 128

HALO = SUBLANES


def _dot(a, b, precision=None):
    return jnp.dot(a, b, preferred_element_type=F32, precision=precision)


def _dot_nt(a, b, precision=None):
    return lax.dot_general(a, b, (((1,), (1,)), ((), ())), preferred_element_type=F32, precision=precision)


def _dot_tn(a, b, precision=None):
    return lax.dot_general(a, b, (((0,), (0,)), ((), ())), preferred_element_type=F32, precision=precision)


def _rms(x, gain):
    return x * lax.rsqrt(jnp.mean(x * x, axis=-1, keepdims=True) + RMS_EPS) * gain


def _sigmoid(x):
    return 1.0 / (1.0 + jnp.exp(-x))


def _silu(x):
    return x * _sigmoid(x)


def _softplus(x):
    return jnp.maximum(x, 0.0) + jnp.log(1.0 + jnp.exp(-jnp.abs(x)))


def _seg64_sum(x):
    lo = lax.broadcasted_iota(jnp.int32, x.shape, 1) < RWKV_DIM
    s_lo = jnp.sum(jnp.where(lo, x, 0.0), axis=-1, keepdims=True)
    s_hi = jnp.sum(jnp.where(lo, 0.0, x), axis=-1, keepdims=True)
    return jnp.where(lo, s_lo, s_hi)


def _shifted_rows(z, scr_ref, carry_ref, first, shifts):
    tm, width = z.shape
    assert width >= 2 * LANES, width
    prev = carry_ref[...]
    scr_ref[0:HALO, :] = jnp.where(first, jnp.zeros_like(prev), prev)
    scr_ref[HALO:HALO + tm, :] = z
    carry_ref[...] = z[tm - HALO:, :]
    return [scr_ref[pl.ds(HALO - s, tm), :] for s in shifts]


def _rope_kernel(pos_ref, invf_ref, c_ref, sa_ref, sb_ref):
    ang = pos_ref[...] * invf_ref[...]
    lane = lax.broadcasted_iota(jnp.int32, ang.shape, 1)
    cos = jnp.cos(ang)
    sin = jnp.sin(ang)
    half = MLA_ROPE // 2
    c_ref[...] = jnp.where(lane < MLA_NOPE, 1.0, jnp.where(lane < MLA_QK, cos, 0.0))
    sa_ref[...] = jnp.where((lane >= MLA_NOPE) & (lane < MLA_NOPE + half), -sin, 0.0)
    sb_ref[...] = jnp.where((lane >= MLA_NOPE + half) & (lane < MLA_QK), sin, 0.0)


def _rope_tables(positions, tm):
    n = positions.size
    pos = jnp.broadcast_to(positions.reshape(n, 1).astype(F32), (n, LANES))
    inv_freq = ROPE_THETA ** (-jnp.arange(0, MLA_ROPE, 2, dtype=F32) / MLA_ROPE)
    half = MLA_ROPE // 2
    invf = jnp.zeros((1, LANES), F32)
    invf = invf.at[0, MLA_NOPE:MLA_NOPE + half].set(inv_freq).at[0, MLA_NOPE + half:MLA_QK].set(inv_freq)
    spec = pl.BlockSpec((tm, LANES), lambda i: (i, 0))
    out = jax.ShapeDtypeStruct((n, LANES), F32)
    return pl.pallas_call(
        _rope_kernel,
        grid=(n // tm,),
        in_specs=[spec, pl.BlockSpec((1, LANES), lambda i: (0, 0))],
        out_specs=[spec, spec, spec],
        out_shape=[out, out, out],
        name="rope_tables",
    )(pos, invf)


def _transpose_cast_kernel(w_ref, o_ref):
    o_ref[...] = w_ref[...].T.astype(o_ref.dtype)


def _transpose_cast(w_t, rows, tn=256):
    d = w_t.shape[1]
    assert rows % tn == 0, (rows, tn)
    return pl.pallas_call(
        _transpose_cast_kernel,
        grid=(rows // tn,),
        in_specs=[pl.BlockSpec((tn, d), lambda i: (i, 0))],
        out_specs=pl.BlockSpec((d, tn), lambda i: (0, i)),
        out_shape=jax.ShapeDtypeStruct((d, rows), BF16),
        name="weight_transpose",
    )(w_t)


def _ev_in_kernel(x_ref, gain_ref, w_ref, ws_ref, ca_ref, cq_ref, alog_ref, dtb_ref,
                  ya_ref, q_ref, k_ref, v_ref, gate_ref, bg_ref,
                  scr_ref, carry_p_ref, carry_qkv_ref):
    first = pl.program_id(1) == 0
    xn = _rms(x_ref[...], gain_ref[...]).astype(BF16)

    za = _dot(xn, w_ref[:, 0:3 * A_WIDTH])
    a_b = za[:, 0:A_WIDTH]
    prod = za[:, A_WIDTH:2 * A_WIDTH] * za[:, 2 * A_WIDTH:3 * A_WIDTH]
    p1, p2 = _shifted_rows(prod, scr_ref.at[:, 0:A_WIDTH], carry_p_ref, first, (1, 2))
    ca = ca_ref[...]
    ya = a_b * (ca[0:1, :] * p2 + ca[1:2, :] * p1 + ca[2:3, :] * prod)
    ya_ref[...] = ya.astype(ya_ref.dtype)

    base = 3 * A_WIDTH
    for g, out_ref in enumerate((q_ref, k_ref, v_ref)):
        cols = slice(base + g * GDN_WIDTH, base + (g + 1) * GDN_WIDTH)
        z = _dot(xn, w_ref[:, cols])
        z1, z2, z3 = _shifted_rows(z, scr_ref, carry_qkv_ref.at[g], first, (1, 2, 3))
        cw = cq_ref[:, g * GDN_WIDTH:(g + 1) * GDN_WIDTH]
        y = _silu(cw[0:1, :] * z3 + cw[1:2, :] * z2 + cw[2:3, :] * z1 + cw[3:4, :] * z)
        for h in range(GDN_HEADS):
            hs = slice(h * GDN_DIM, (h + 1) * GDN_DIM)
            yh = y[:, hs]
            if g < 2:
                yh = yh * lax.rsqrt(jnp.sum(yh * yh, axis=-1, keepdims=True) + L2_EPS)
            if g == 0:
                yh = yh * (GDN_DIM ** -0.5)
            out_ref[:, hs] = yh

    gcols = slice(base + 3 * GDN_WIDTH, base + 4 * GDN_WIDTH)
    gate_ref[...] = _dot(xn, w_ref[:, gcols])

    zs = _dot(xn, ws_ref[...])
    lane = lax.broadcasted_iota(jnp.int32, zs.shape, 1)
    g_log = -jnp.exp(alog_ref[...]) * _softplus(zs + dtb_ref[...])
    bg_ref[...] = jnp.where(lane < GDN_HEADS, _sigmoid(zs), g_log)


def _ev_in(x2, gain, w_main, w_small, conv_a, conv_qkv, alog_l, dtb_l, bsz, tlen, tm):
    n, d = x2.shape
    nt = tlen // tm
    row = lambda b, i: (b * nt + i, 0)
    const = lambda b, i: (0, 0)
    rs = lambda w: pl.BlockSpec((tm, w), row)
    full = lambda a: pl.BlockSpec(a.shape, const)
    outs = [
        jax.ShapeDtypeStruct((n, A_WIDTH), BF16),
        jax.ShapeDtypeStruct((n, GDN_WIDTH), F32),
        jax.ShapeDtypeStruct((n, GDN_WIDTH), F32),
        jax.ShapeDtypeStruct((n, GDN_WIDTH), F32),
        jax.ShapeDtypeStruct((n, GDN_WIDTH), F32),
        jax.ShapeDtypeStruct((n, LANES), F32),
    ]
    return pl.pallas_call(
        _ev_in_kernel,
        grid=(bsz, nt),
        in_specs=[rs(d), full(gain), full(w_main), full(w_small), full(conv_a), full(conv_qkv),
                  full(alog_l), full(dtb_l)],
        out_specs=[rs(A_WIDTH), rs(GDN_WIDTH), rs(GDN_WIDTH), rs(GDN_WIDTH), rs(GDN_WIDTH), rs(LANES)],
        out_shape=outs,
        scratch_shapes=[
            pltpu.VMEM((HALO + tm, GDN_WIDTH), F32),
            pltpu.VMEM((HALO, A_WIDTH), F32),
            pltpu.VMEM((3, HALO, GDN_WIDTH), F32),
        ],
        compiler_params=pltpu.CompilerParams(
            dimension_semantics=("arbitrary", "arbitrary"), vmem_limit_bytes=VMEM_LIMIT),
        name="even_in_proj",
    )(x2, gain, w_main, w_small, conv_a, conv_qkv, alog_l, dtb_l)


def _unit_lower_inverse_many(l_mats, row, col, dot):
    eye = (row == col).astype(F32)
    in_block = (row // GDN_BLOCK) == (col // GDN_BLOCK)
    l_diag = [jnp.where(in_block, m, 0.0) for m in l_mats]
    l_off = [jnp.where(in_block, 0.0, m) for m in l_mats]
    x2 = [dot(m, m) for m in l_diag]
    x4 = [dot(m, m) for m in x2]
    x8 = [dot(m, m) for m in x4]
    d_inv = [eye - m for m in l_diag]
    d_inv = [m + dot(m, x) for m, x in zip(d_inv, x2)]
    d_inv = [m + dot(m, x) for m, x in zip(d_inv, x4)]
    d_inv = [m + dot(m, x) for m, x in zip(d_inv, x8)]
    n1 = [dot(m, x) for m, x in zip(d_inv, l_off)]
    n2 = [dot(m, m) for m in n1]
    outer = [eye - m for m in n1]
    outer = [m + dot(m, x) for m, x in zip(outer, n2)]
    return [dot(m, x) for m, x in zip(outer, d_inv)]


def _gdn_kernel(q_ref, k_ref, v_ref, bg_ref, gate_ref, onorm_ref, o_ref, s_ref, *, nchunk):
    @pl.when(pl.program_id(1) == 0)
    def _():
        s_ref[...] = jnp.zeros_like(s_ref)

    c = GDN_CHUNK
    dk = GDN_DIM
    pairs = range(GDN_HEADS // 2)
    row_c = lax.broadcasted_iota(jnp.int32, (c, c), 0)
    col_c = lax.broadcasted_iota(jnp.int32, (c, c), 1)
    tri = (row_c >= col_c).astype(F32)
    row1 = lax.broadcasted_iota(jnp.int32, (2 * c, 2 * c), 0)
    col1 = lax.broadcasted_iota(jnp.int32, (2 * c, 2 * c), 1)
    same_head = (row1 // c) == (col1 // c)
    lag = (row1 % c) - (col1 % c)
    causal = same_head & (lag >= 0)
    strict = same_head & (lag > 0)
    tri_dup = (lax.broadcasted_iota(jnp.int32, (c, 2 * c), 0) <= lax.broadcasted_iota(jnp.int32, (c, 2 * c), 1) % c).astype(F32)
    first_half = lax.broadcasted_iota(jnp.int32, (1, 2 * c), 1) < c
    head0_rows = (lax.broadcasted_iota(jnp.int32, (2 * c, 1), 0) < c).astype(F32)
    onorm = onorm_ref[...]

    def bdot(a, b):
        return _dot(a.astype(BF16), b.astype(BF16))

    def rows2(a, b):
        return jnp.concatenate([a, b], axis=0)

    items = [(ci, p) for ci in range(nchunk) for p in pairs]
    q_s, k_s, v_s, beta_s, gc_s, gl_s, l_mats, qk = [], [], [], [], [], [], [], []
    for ci in range(nchunk):
        rows = slice(ci * c, (ci + 1) * c)
        bg = bg_ref[rows, :]
        g_cols = _dot(tri, bg, precision=HIGHEST)
        g_rows = _dot_tn(bg, tri_dup, precision=HIGHEST)
        for p in pairs:
            h0, h1 = 2 * p, 2 * p + 1
            s0, s1 = slice(h0 * dk, (h0 + 1) * dk), slice(h1 * dk, (h1 + 1) * dk)
            q_s.append(rows2(q_ref[rows, s0], q_ref[rows, s1]))
            k_s.append(rows2(k_ref[rows, s0], k_ref[rows, s1]))
            v_s.append(rows2(v_ref[rows, s0], v_ref[rows, s1]))
            beta_s.append(rows2(bg[:, h0:h0 + 1], bg[:, h1:h1 + 1]))
            gc0 = g_cols[:, GDN_HEADS + h0:GDN_HEADS + h0 + 1]
            gc1 = g_cols[:, GDN_HEADS + h1:GDN_HEADS + h1 + 1]
            gc_s.append(rows2(gc0, gc1))
            gl_s.append(rows2(jnp.broadcast_to(gc0[c - 1:c, :], (c, 1)), jnp.broadcast_to(gc1[c - 1:c, :], (c, 1))))
            gr = jnp.where(first_half, g_rows[GDN_HEADS + h0:GDN_HEADS + h0 + 1, :], g_rows[GDN_HEADS + h1:GDN_HEADS + h1 + 1, :])
            diff = gc_s[-1] - gr
            decay = jnp.where(causal, jnp.exp(jnp.where(causal, diff, 0.0)), 0.0)
            kb = k_s[-1] * beta_s[-1]
            score = _dot_nt(rows2(kb, q_s[-1]).astype(BF16), k_s[-1].astype(BF16))
            l_mats.append(jnp.where(strict, score[0:2 * c] * decay, 0.0))
            qk.append(score[2 * c:4 * c] * decay)
    t_inv = _unit_lower_inverse_many(l_mats, row1, col1, bdot)
    eg = [jnp.exp(g) for g in gc_s]
    wu = [bdot(t_inv[i], jnp.concatenate([k_s[i] * (beta_s[i] * eg[i]), v_s[i] * beta_s[i]], axis=1)) for i in range(len(items))]
    qy = [bdot(qk[i], wu[i]) for i in range(len(items))]
    gd = []
    for i in range(len(items)):
        k_dec = k_s[i] * jnp.exp(gl_s[i] - gc_s[i])
        wu0 = wu[i] * head0_rows
        wu1 = wu[i] - wu0
        expanded = jnp.concatenate([wu0[:, 0:dk], wu1[:, 0:dk], wu0[:, dk:], wu1[:, dk:]], axis=1)
        gd.append(_dot_tn(k_dec.astype(BF16), expanded.astype(BF16)))

    for i, (ci, p) in enumerate(items):
        rows = slice(ci * c, (ci + 1) * c)
        q_hat = q_s[i] * eg[i] - qy[i][:, 0:dk]
        y0 = qy[i][:, dk:]
        for hh in range(2):
            h = 2 * p + hh
            hs = slice(h * dk, (h + 1) * dk)
            hr = slice(hh * c, (hh + 1) * c)
            state = s_ref[h]
            o = bdot(q_hat[hr], state) + y0[hr]
            g_last = gl_s[i][hh * c:hh * c + 1, :]
            s_ref[h] = state * jnp.exp(g_last) - bdot(gd[i][:, hh * dk:(hh + 1) * dk], state) + gd[i][:, (2 + hh) * dk:(3 + hh) * dk]
            o = _rms(o, onorm) * _silu(gate_ref[rows, hs])
            o_ref[rows, hs] = o.astype(o_ref.dtype)


def _gdn(q, k, v, bg, gate, onorm, bsz, tlen, tt):
    n = q.shape[0]
    nt = tlen // tt
    nchunk = tt // GDN_CHUNK
    row = lambda b, i: (b * nt + i, 0)
    rs = lambda w: pl.BlockSpec((tt, w), row)
    return pl.pallas_call(
        functools.partial(_gdn_kernel, nchunk=nchunk),
        grid=(bsz, nt),
        in_specs=[rs(GDN_WIDTH), rs(GDN_WIDTH), rs(GDN_WIDTH), rs(LANES),
                  rs(GDN_WIDTH), pl.BlockSpec(onorm.shape, lambda b, i: (0, 0))],
        out_specs=rs(GDN_WIDTH),
        out_shape=jax.ShapeDtypeStruct((n, GDN_WIDTH), BF16),
        scratch_shapes=[pltpu.VMEM((GDN_HEADS, GDN_DIM, GDN_DIM), F32)],
        compiler_params=pltpu.CompilerParams(
            dimension_semantics=("arbitrary", "arbitrary"), vmem_limit_bytes=VMEM_LIMIT),
        name="gated_delta_rule",
    )(q, k, v, bg, gate, onorm)


OD_R = 0
OD_K = RWKV_WIDTH
OD_V = 2 * RWKV_WIDTH
OD_WA = 3 * RWKV_WIDTH
OD_G = OD_WA + LANES
OD_X = RWKV_SHIFT_COLS
OD_SMALL = 3 * LANES
OD_CQ = OD_X + LANES
OD_CKV = OD_CQ + MLA_Q_LORA
OD_COLS = OD_CKV + MLA_KV_LORA


def _od_in_kernel(*refs, has_vres):
    (x_ref, gain_ref, w_ref, mu_ref, wa2_ref, w0_ref, a0_ref, g2_ref, kk_ref, ka_ref,
     qan_ref, kvan_ref, wuq_ref, wuk_ref, wuv_ref, qln_ref, kln_ref, rot_ref, seg_ref,
     c_ref, sa_ref, sb_ref) = refs[:22]
    pos = 22
    if has_vres:
        vfirst_ref, v0_ref, v2_ref = refs[pos:pos + 3]
        pos += 3
    (r_out, lw_out, k_out, v_out, kkn_out, b_out, g_out, q_out, kmla_out, vmla_out) = refs[pos:pos + 10]
    scr_ref, carry_ref = refs[pos + 10:]

    first = pl.program_id(1) == 0
    xn = _rms(x_ref[...], gain_ref[...]).astype(BF16)

    def lerped(start, width, slot):
        z = _dot(xn, w_ref[:, start:start + width])
        (zp,) = _shifted_rows(z, scr_ref.at[:, 0:width], carry_ref.at[:, slot:slot + width], first, (1,))
        return z, z + mu_ref[:, start:start + width] * (zp - z)

    small, small_l = lerped(OD_WA, OD_SMALL, OD_WA)
    wa = small_l[:, 0:LANES]
    gd = small_l[:, LANES:2 * LANES]
    zx = small[:, 2 * LANES:3 * LANES]
    zx_l = small_l[:, 2 * LANES:3 * LANES]

    two = lambda t: jnp.concatenate([t, t], axis=1)
    cos2 = two(c_ref[...])
    sin2 = two(sb_ref[...] - sa_ref[...])

    cq = _dot(xn, w_ref[:, OD_CQ:OD_CQ + MLA_Q_LORA])
    q_all = _dot(_rms(cq, qan_ref[...]).astype(BF16), wuq_ref[...])
    ckv = _dot(xn, w_ref[:, OD_CKV:OD_CKV + MLA_KV_LORA])
    ckv_n = _rms(ckv, kvan_ref[...]).astype(BF16)
    k_all = _dot(ckv_n, wuk_ref[...])
    vmla_out[...] = _dot(ckv_n, wuv_ref[...]).astype(vmla_out.dtype)
    lane = lax.broadcasted_iota(jnp.int32, zx.shape, 1)
    k_rope2 = two(jnp.where((lane >= MLA_NOPE) & (lane < MLA_QK), pltpu.roll(zx, MLA_NOPE, 1), 0.0))
    q_ln2 = two(qln_ref[...])
    k_ln2 = two(kln_ref[...])
    scale = MLA_QK ** -0.5

    def norm_rope(t, gain2):
        ms = _dot((t * t).astype(BF16), seg_ref[...]) * (1.0 / MLA_QK)
        t = t * lax.rsqrt(ms + RMS_EPS) * gain2
        return t * cos2 + _dot(t.astype(BF16), rot_ref[...]) * sin2

    def mla_heads(pairs):
        for p in pairs:
            hs = slice(2 * p * MLA_PAD, (2 * p + 2) * MLA_PAD)
            q_out[:, hs] = (norm_rope(q_all[:, hs], q_ln2) * scale).astype(q_out.dtype)
            kmla_out[:, hs] = norm_rope(k_all[:, hs] + k_rope2, k_ln2).astype(kmla_out.dtype)

    _, r = lerped(OD_R, RWKV_WIDTH, OD_R)
    r_out[...] = r
    mla_heads((0,))

    lora_in = jnp.where(lane < RWKV_W_LORA, jnp.tanh(wa), wa).astype(BF16)
    lora = _dot(lora_in, wa2_ref[...])
    w_log = -_softplus(-(w0_ref[...] + lora[:, 0:RWKV_WIDTH])) - 0.5
    lw_out[...] = -jnp.exp(w_log)
    a = _sigmoid(a0_ref[...] + lora[:, RWKV_WIDTH:2 * RWKV_WIDTH])
    mla_heads((1, 2))

    _, k = lerped(OD_K, RWKV_WIDTH, OD_K)
    kx = k * kk_ref[...]
    for p in range(RWKV_PAIRS):
        ps = slice(p * LANES, (p + 1) * LANES)
        kxp = kx[:, ps]
        kkp = kxp * lax.rsqrt(_seg64_sum(kxp * kxp) + L2_EPS)
        kkn_out[:, ps] = kkp
        b_out[:, ps] = kkp * a[:, ps]
    k_out[...] = k * (1.0 + (a - 1.0) * ka_ref[...])
    mla_heads((3,))

    _, v = lerped(OD_V, RWKV_WIDTH, OD_V)
    if has_vres:
        mix = _sigmoid(v0_ref[...] + _dot(zx_l.astype(BF16), v2_ref[...]))
        v = v + (vfirst_ref[...] - v) * mix
    v_out[...] = v
    g_out[...] = _dot(_sigmoid(gd).astype(BF16), g2_ref[...])


def _od_in(x2, gain, w_in, mu, wa2, w0, a0, g2, k_k, k_a, qan, kvan, wuq, wuk, wuv, qln, kln, rot2, seg2,
           tables, vres, bsz, tlen, tm):
    n, d = x2.shape
    nt = tlen // tm
    row = lambda b, i: (b * nt + i, 0)
    const = lambda b, i: (0, 0)
    rs = lambda w: pl.BlockSpec((tm, w), row)
    full = lambda a: pl.BlockSpec(a.shape, const)
    params = [gain, w_in, mu, wa2, w0, a0, g2, k_k, k_a, qan, kvan, wuq, wuk, wuv, qln, kln, rot2, seg2]
    args = [x2] + params + list(tables)
    in_specs = [rs(d)] + [full(a) for a in params] + [rs(LANES)] * 3
    if vres is not None:
        v_first, v0, v2 = vres
        args += [v_first, v0, v2]
        in_specs += [rs(RWKV_WIDTH), full(v0), full(v2)]
    f32w = lambda w: jax.ShapeDtypeStruct((n, w), F32)
    bfw = lambda w: jax.ShapeDtypeStruct((n, w), BF16)
    outs = [f32w(RWKV_WIDTH)] * 7 + [bfw(MLA_HEADS * MLA_PAD), bfw(MLA_HEADS * MLA_PAD), bfw(MLA_HEADS * MLA_V)]
    out_specs = [rs(RWKV_WIDTH)] * 7 + [rs(MLA_HEADS * MLA_PAD), rs(MLA_HEADS * MLA_PAD), rs(MLA_HEADS * MLA_V)]
    carry_w = RWKV_SHIFT_COLS + LANES
    return pl.pallas_call(
        functools.partial(_od_in_kernel, has_vres=vres is not None),
        grid=(bsz, nt),
        in_specs=in_specs,
        out_specs=out_specs,
        out_shape=outs,
        scratch_shapes=[pltpu.VMEM((HALO + tm, RWKV_WIDTH), F32), pltpu.VMEM((HALO, carry_w), F32)],
        compiler_params=pltpu.CompilerParams(
            dimension_semantics=("arbitrary", "arbitrary"), vmem_limit_bytes=VMEM_LIMIT),
        name="odd_in_proj",
    )(*args)


def _rwkv_kernel(r_ref, lw_ref, k_ref, v_ref, kk_ref, b_ref, g_ref, rk_ref, lnw_ref, lnb_ref,
                 o_ref, h_ref, y_ref, *, nchunk):
    @pl.when(pl.program_id(1) == 0)
    def _():
        h_ref[...] = jnp.zeros_like(h_ref)

    c = RWKV_CHUNK
    d = RWKV_DIM
    lane = lax.broadcasted_iota(jnp.int32, (1, LANES), 1)
    m_lo = (lane < d).astype(F32)
    m_hi = 1.0 - m_lo
    row_c = lax.broadcasted_iota(jnp.int32, (c, c), 0)
    col_c = lax.broadcasted_iota(jnp.int32, (c, c), 1)
    tri = (row_c >= col_c).astype(F32)
    row1 = lax.broadcasted_iota(jnp.int32, (2 * c, 2 * c), 0)
    col1 = lax.broadcasted_iota(jnp.int32, (2 * c, 2 * c), 1)
    eye1 = row1 == col1
    row2 = lax.broadcasted_iota(jnp.int32, (4 * c, 4 * c), 0)
    col2 = lax.broadcasted_iota(jnp.int32, (4 * c, 4 * c), 1)
    same_head = ((row2 // c) % 2) == ((col2 // c) % 2)
    min_lag = jnp.where(row2 < 2 * c, 1, 0)
    keep = same_head & ((row2 % c) - (col2 % c) >= min_lag)

    def bdot(a, b):
        return _dot(a.astype(BF16), b.astype(BF16))

    def stack(x):
        return jnp.concatenate([x * m_lo, x * m_hi], axis=0)

    pairs = range(RWKV_PAIRS)
    ps = [slice(p * LANES, (p + 1) * LANES) for p in pairs]
    zero = jnp.zeros((2 * c, LANES), F32)
    items = [(ci, p) for ci in range(nchunk) for p in pairs]
    n_items = range(len(items))
    qt_p, w_last_p, am_s, v_s, lhs1, rhs1, lhs_t = [], [], [], [], [], [], []
    for ci in range(nchunk):
        rows = slice(ci * c, (ci + 1) * c)
        lw = lw_ref[rows, :]
        logw = _dot(tri, lw, precision=HIGHEST)
        logw_last = logw[c - 1:c, :]
        w_inv = jnp.exp(-logw)
        w_last = jnp.exp(logw_last)
        to_end = jnp.exp(logw_last - logw)
        am = kk_ref[rows, :] * jnp.exp(logw - lw)
        qt = r_ref[rows, :] * jnp.exp(logw)
        bp = b_ref[rows, :] * w_inv
        kp = k_ref[rows, :] * w_inv
        bh = b_ref[rows, :] * to_end
        kh = k_ref[rows, :] * to_end
        v = v_ref[rows, :]
        for s in ps:
            qt_p.append(qt[:, s])
            w_last_p.append(w_last[:, s])
            am_s.append(stack(am[:, s]))
            v_s.append(stack(v[:, s]))
            lhs1.append(jnp.concatenate([am_s[-1], stack(qt[:, s])], axis=0).astype(BF16))
            rhs1.append(jnp.concatenate([bp[:, s], bp[:, s], kp[:, s], kp[:, s]], axis=0).astype(BF16))
            lhs_t.append(jnp.concatenate([stack(bh[:, s]), stack(kh[:, s])], axis=0).astype(BF16))
    score = [jnp.where(keep, _dot_nt(lhs1[i], rhs1[i]), 0.0) for i in n_items]
    t_inv = _unit_lower_inverse_many([sc[0:2 * c, 0:2 * c] for sc in score], row1, col1, bdot)
    r1 = [bdot(score[i][0:2 * c, 2 * c:4 * c], v_s[i]) for i in n_items]
    x = [bdot(t_inv[i], jnp.concatenate([am_s[i], r1[i]], axis=1)) for i in n_items]
    rhs2 = [jnp.concatenate([jnp.concatenate([x[i][:, 0:LANES], -x[i][:, LANES:]], axis=1),
                             jnp.concatenate([zero, v_s[i]], axis=1)], axis=0).astype(BF16) for i in n_items]
    qy = [_dot(score[i][2 * c:4 * c, :].astype(BF16), rhs2[i]) for i in n_items]
    gd = [_dot_tn(lhs_t[i], rhs2[i]) for i in n_items]

    for i, (ci, p) in enumerate(items):
        rows = slice(ci * c, (ci + 1) * c)
        q_hat = qt_p[i] - (qy[i][0:c, 0:LANES] + qy[i][c:2 * c, 0:LANES])
        y0 = qy[i][0:c, LANES:] + qy[i][c:2 * c, LANES:]
        h = h_ref[p]
        y_ref[rows, ps[p]] = bdot(q_hat, h) + y0
        wl_col = jnp.sum(jnp.where(eye1, w_last_p[i], 0.0), axis=1, keepdims=True)
        h_ref[p] = wl_col * h - bdot(gd[i][:, 0:LANES], h) + gd[i][:, LANES:]

    for p in pairs:
        s = slice(p * LANES, (p + 1) * LANES)
        y = y_ref[:, s]
        mu = _seg64_sum(y) * (1.0 / d)
        yc = y - mu
        var = _seg64_sum(yc * yc) * (1.0 / d)
        y = yc * lax.rsqrt(var + RWKV_LN_EPS) * lnw_ref[:, s] + lnb_ref[:, s]
        bonus = _seg64_sum(r_ref[:, s] * k_ref[:, s] * rk_ref[:, s])
        y = (y + bonus * v_ref[:, s]) * g_ref[:, s]
        o_ref[:, s] = y.astype(o_ref.dtype)


def _rwkv(r, lw, k, v, kk, b, g, rk, lnw, lnb, bsz, tlen, tt):
    n, width = r.shape
    nt = tlen // tt
    blk = pl.BlockSpec((tt, width), lambda bb, i: (bb * nt + i, 0))
    par = pl.BlockSpec((1, width), lambda bb, i: (0, 0))
    return pl.pallas_call(
        functools.partial(_rwkv_kernel, nchunk=tt // RWKV_CHUNK),
        grid=(bsz, nt),
        in_specs=[blk] * 7 + [par] * 3,
        out_specs=blk,
        out_shape=jax.ShapeDtypeStruct((n, width), BF16),
        scratch_shapes=[pltpu.VMEM((RWKV_PAIRS, LANES, LANES), F32), pltpu.VMEM((tt, width), F32)],
        compiler_params=pltpu.CompilerParams(
            dimension_semantics=("arbitrary", "arbitrary"), vmem_limit_bytes=VMEM_LIMIT),
        name="rwkv7_chunked",
    )(r, lw, k, v, kk, b, g, rk, lnw, lnb)


def _mla_kernel(q_ref, k_ref, v_ref, o_ref, m_ref, acc_ref, *, tq, tk, nh):
    qi = pl.program_id(2)
    ntile = tk // LANES
    row = lax.broadcasted_iota(jnp.int32, (tq, LANES), 0)
    lane = lax.broadcasted_iota(jnp.int32, (tq, LANES), 1)
    own = [lax.broadcasted_iota(jnp.int32, (tk, LANES), 1) < MLA_V,
           lax.broadcasted_iota(jnp.int32, (tk, LANES), 1) >= MLA_V]
    ones = jnp.ones((tk, LANES), BF16)

    def block(h, q, j, masked, r0=0):
        hs = slice(h * MLA_PAD, (h + 1) * MLA_PAD)
        nrow = tq - r0
        start = pl.multiple_of(j * tk, tk)
        s = _dot_nt(q, k_ref[pl.ds(start, tk), hs])
        tiles = [s[:, i * LANES:(i + 1) * LANES] for i in range(ntile)]
        if masked:
            row_l = lax.broadcasted_iota(jnp.int32, (nrow, LANES), 0)
            lane_l = lax.broadcasted_iota(jnp.int32, (nrow, LANES), 1)
            tiles = [jnp.where(lane_l + i * LANES <= row_l, t, -jnp.inf) for i, t in enumerate(tiles)]
        m_prev = m_ref[h, r0:, :]
        m_tile = tiles[0]
        for t in tiles[1:]:
            m_tile = jnp.maximum(m_tile, t)
        m_new = jnp.maximum(m_prev, jnp.max(m_tile, axis=-1, keepdims=True))
        alpha = jnp.exp(m_prev - m_new)
        p = jnp.concatenate([jnp.exp((t - m_new).astype(BF16)) for t in tiles], axis=1)
        v_ext = jnp.where(own[h % 2], v_ref[pl.ds(start, tk), (h // 2) * LANES:(h // 2 + 1) * LANES], ones)
        acc_ref[h, r0:, :] = alpha * acc_ref[h, r0:, :] + _dot(p, v_ext)
        m_ref[h, r0:, :] = m_new

    ratio = tq // tk
    qs = [q_ref[:, h * MLA_PAD:(h + 1) * MLA_PAD] for h in range(nh)]
    for h in range(nh):
        m_ref[h] = jnp.full((tq, LANES), -jnp.inf, F32)
        acc_ref[h] = jnp.zeros((tq, LANES), F32)

    def full_blocks(j, count):
        for u in range(count):
            for h in range(nh):
                block(h, qs[h], j + u, False)

    n_full = qi * ratio

    def two_blocks(j2, carry):
        full_blocks(2 * j2, 2)
        return carry

    lax.fori_loop(0, n_full // 2, two_blocks, 0)

    @pl.when(n_full % 2 == 1)
    def _():
        full_blocks(n_full - 1, 1)

    for d in range(ratio):
        for h in range(nh):
            block(h, q_ref[d * tk:, h * MLA_PAD:(h + 1) * MLA_PAD], qi * ratio + d, True, d * tk)
    outs = [acc_ref[h] / pltpu.roll(acc_ref[h], MLA_V, 1) for h in range(nh)]
    for p in range(nh // 2):
        o_ref[:, p * LANES:(p + 1) * LANES] = jnp.where(lane < MLA_V, outs[2 * p], outs[2 * p + 1]).astype(o_ref.dtype)


def _mla(q, k, v, bsz, tlen, tq, tk, nh=4):
    n = q.shape[0]
    nq = tlen // tq
    hp = MLA_HEADS // nh
    stat = pltpu.VMEM((nh, tq, LANES), F32)
    return pl.pallas_call(
        functools.partial(_mla_kernel, tq=tq, tk=tk, nh=nh),
        grid=(bsz, hp, nq),
        in_specs=[pl.BlockSpec((tq, nh * MLA_PAD), lambda b, h, i: (b * nq + i, h)),
                  pl.BlockSpec((tlen, nh * MLA_PAD), lambda b, h, i: (b, h)),
                  pl.BlockSpec((tlen, nh * MLA_V), lambda b, h, i: (b, h))],
        out_specs=pl.BlockSpec((tq, nh * MLA_V), lambda b, h, i: (b * nq + i, h)),
        out_shape=jax.ShapeDtypeStruct((n, MLA_HEADS * MLA_V), BF16),
        scratch_shapes=[stat, stat],
        compiler_params=pltpu.CompilerParams(
            dimension_semantics=("arbitrary", "arbitrary", "arbitrary"), vmem_limit_bytes=VMEM_LIMIT),
        name="mla_attention",
    )(q, k, v)


def _ffn_kernel(x_ref, ma_ref, mb_ref, woa_ref, wob_ref, gain_ref, wgu_ref, wd_ref, o_ref, h_ref, *, fc):
    d_ff = wd_ref.shape[0]
    x1 = x_ref[...] + _dot(ma_ref[...], woa_ref[...]) + _dot(mb_ref[...], wob_ref[...])
    xn = _rms(x1, gain_ref[...]).astype(BF16)
    for c in range(d_ff // fc):
        gate = _dot(xn, wgu_ref[:, c * fc:(c + 1) * fc])
        up = _dot(xn, wgu_ref[:, d_ff + c * fc:d_ff + (c + 1) * fc])
        h_ref[:, c * fc:(c + 1) * fc] = (_silu(gate) * up).astype(BF16)
    o_ref[...] = x1 + _dot(h_ref[...], wd_ref[...])


def _ffn(x2, mix_a, mix_b, wo_a, wo_b, gain, w_gu, w_down, tm, fc):
    n, d = x2.shape
    d_ff = w_down.shape[0]
    row = lambda i: (i, 0)
    const = lambda i: (0, 0)
    rs = lambda w: pl.BlockSpec((tm, w), row)
    full = lambda a: pl.BlockSpec(a.shape, const, pipeline_mode=pl.Buffered(1))
    return pl.pallas_call(
        functools.partial(_ffn_kernel, fc=fc),
        grid=(n // tm,),
        in_specs=[rs(d), rs(mix_a.shape[1]), rs(mix_b.shape[1]), full(wo_a), full(wo_b), full(gain),
                  full(w_gu), full(w_down)],
        out_specs=rs(d),
        out_shape=jax.ShapeDtypeStruct((n, d), F32),
        scratch_shapes=[pltpu.VMEM((tm, d_ff), BF16)],
        compiler_params=pltpu.CompilerParams(
            dimension_semantics=("arbitrary",), vmem_limit_bytes=VMEM_LIMIT),
        name="outproj_ffn",
    )(x2, mix_a, mix_b, wo_a, wo_b, gain, w_gu, w_down)


def _row(v):
    return v.reshape(1, -1).astype(F32)


def _pad_lanes(v, width, offset=0):
    out = jnp.zeros((1, width), F32)
    return out.at[0, offset:offset + v.shape[-1]].set(v.astype(F32))


def _pad_heads(w, heads, src, dst):
    rows = w.shape[0]
    w = w.reshape(rows, heads, src)
    return jnp.pad(w, ((0, 0), (0, 0), (0, dst - src))).reshape(rows, heads * dst)


def _tile(tlen, want):
    t = min(want, tlen)
    assert tlen % t == 0, (tlen, t)
    return t


def kernel(x, positions, norm_mix, norm_ffn, ffn_w_gu, ffn_w_down, ev_w_in, ev_conv_a, ev_conv_qkv, ev_a_log, ev_dt_bias, ev_out_norm, ev_w_out, od_w_in, od_shift_mu, od_w0, od_w2, od_a0, od_a2, od_g2, od_k_k, od_k_a, od_r_k, od_lnx_w, od_lnx_b, od_vres_w1, od_vres_mu, od_vres_v0, od_vres_v2, od_qa_norm, od_kva_norm, od_w_uq, od_w_ukv, od_q_ln, od_k_ln, od_w_out):
    bsz, tlen, d = x.shape
    n = bsz * tlen
    depth = norm_mix.shape[0]
    tm_proj = _tile(tlen, 256)
    tm_ffn = _tile(tlen, 512)
    tt_gdn = _tile(tlen, 512)
    tt_rwkv = _tile(tlen, 512)
    tq_mla = _tile(tlen, 1024)
    tk_mla = _tile(tq_mla, 512)
    fc = 256

    x2 = x.reshape(n, d)
    tables = _rope_tables(positions, tm_proj)
    half = MLA_ROPE // 2
    idx = jnp.arange(MLA_NOPE, MLA_NOPE + half)
    rot = jnp.zeros((MLA_PAD, MLA_PAD), F32).at[idx + half, idx].set(-1.0).at[idx, idx + half].set(1.0)
    eye2 = jnp.eye(2, dtype=F32)
    rot2 = jnp.kron(eye2, rot).astype(BF16)
    seg2 = jnp.kron(eye2, jnp.ones((MLA_PAD, MLA_PAD), F32)).astype(BF16)
    v_first = None

    for layer in range(depth):
        gain = _row(norm_mix[layer])
        if layer % 2 == 0:
            e = layer // 2
            n_main = 3 * A_WIDTH + 4 * GDN_WIDTH
            w_t = jnp.swapaxes(ev_w_in[e], 0, 1)
            w_main = _transpose_cast(w_t, n_main)
            w_small = jnp.zeros((d, LANES), F32).at[:, :2 * GDN_HEADS].set(jnp.swapaxes(w_t[n_main:], 0, 1)).astype(BF16)
            alog_l = _pad_lanes(ev_a_log[e], LANES, GDN_HEADS)
            dtb_l = _pad_lanes(ev_dt_bias[e], LANES, GDN_HEADS)
            ya, q, k, v, gate, bg = _ev_in(x2, gain, w_main, w_small, ev_conv_a[e].astype(F32),
                                           ev_conv_qkv[e].astype(F32), alog_l, dtb_l, bsz, tlen, tm_proj)
            o = _gdn(q, k, v, bg, gate, _row(ev_out_norm[e]), bsz, tlen, tt_gdn)
            mix_a, mix_b = ya, o
            w_out = ev_w_out[e].astype(BF16)
            wo_a, wo_b = w_out[:A_WIDTH], w_out[A_WIDTH:]
        else:
            o = layer // 2
            has_vres = o > 0
            w_in = od_w_in[o]
            extra = jnp.zeros((d, LANES), F32).at[:, :MLA_ROPE].set(w_in[:, RWKV_SHIFT_COLS + MLA_Q_LORA + MLA_KV_LORA:])
            mu = jnp.zeros((1, OD_COLS), F32).at[0, :RWKV_SHIFT_COLS].set(od_shift_mu[o])
            if has_vres:
                extra = extra.at[:, MLA_ROPE:MLA_ROPE + RWKV_V_LORA].set(od_vres_w1[o - 1])
                mu = mu.at[0, OD_X + MLA_ROPE:OD_X + MLA_ROPE + RWKV_V_LORA].set(od_vres_mu[o - 1])
            n_mla = MLA_Q_LORA + MLA_KV_LORA
            w_packed = jnp.concatenate(
                [w_in[:, :RWKV_SHIFT_COLS], extra, w_in[:, RWKV_SHIFT_COLS:RWKV_SHIFT_COLS + n_mla]], axis=1).astype(BF16)
            wa2 = jnp.zeros((LANES, 2 * RWKV_WIDTH), F32)
            wa2 = wa2.at[:RWKV_W_LORA, :RWKV_WIDTH].set(od_w2[o]).at[RWKV_W_LORA:, RWKV_WIDTH:].set(od_a2[o]).astype(BF16)
            wuq = _pad_heads(od_w_uq[o], MLA_HEADS, MLA_QK, MLA_PAD).astype(BF16)
            w_ukv = od_w_ukv[o].reshape(MLA_KV_LORA, MLA_HEADS, MLA_NOPE + MLA_V)
            wuk = _pad_heads(w_ukv[:, :, :MLA_NOPE].reshape(MLA_KV_LORA, -1), MLA_HEADS, MLA_NOPE, MLA_PAD).astype(BF16)
            wuv = w_ukv[:, :, MLA_NOPE:].reshape(MLA_KV_LORA, -1).astype(BF16)
            vres = None
            if has_vres:
                v2 = jnp.zeros((LANES, RWKV_WIDTH), F32).at[MLA_ROPE:MLA_ROPE + RWKV_V_LORA].set(od_vres_v2[o - 1])
                vres = (v_first, _row(od_vres_v0[o - 1]), v2.astype(BF16))
            (r, lw, k, v, kk, b, g, q_mla, k_mla, v_mla) = _od_in(
                x2, gain, w_packed, mu, wa2, _row(od_w0[o]), _row(od_a0[o]), od_g2[o].astype(BF16),
                _row(od_k_k[o]), _row(od_k_a[o]), _row(od_qa_norm[o]), _row(od_kva_norm[o]), wuq, wuk, wuv,
                _pad_lanes(od_q_ln[o], MLA_PAD), _pad_lanes(od_k_ln[o], MLA_PAD), rot2, seg2, tables, vres, bsz, tlen, tm_proj)
            if not has_vres:
                v_first = v
            y_rwkv = _rwkv(r, lw, k, v, kk, b, g, _row(od_r_k[o]), _row(od_lnx_w[o]), _row(od_lnx_b[o]),
                           bsz, tlen, tt_rwkv)
            o_mla = _mla(q_mla, k_mla, v_mla, bsz, tlen, tq_mla, tk_mla)
            mix_a, mix_b = y_rwkv, o_mla
            w_out = od_w_out[o].astype(BF16)
            wo_a, wo_b = w_out[:RWKV_WIDTH], w_out[RWKV_WIDTH:]
        x2 = _ffn(x2, mix_a, mix_b, wo_a, wo_b, _row(norm_ffn[layer]), ffn_w_gu[layer].astype(BF16),
                  ffn_w_down[layer].astype(BF16), tm_ffn, fc)
    return x2.reshape(bsz, tlen, d)
```
